```python
import jax, jax.numpy as jnp
from jax import lax
import numpy as np

D_MODEL = 1024
BATCH = 8
SEQ = 8192
DEPTH = 2
DEC_BATCH = 32
DEC_SEQ = 64
PAST_LEN = 4096

CHUNK = 64
MLP_CHUNK = 128
HEAD_DIM = 64
D_A = 512
N_HEADS_A = D_A // HEAD_DIM
D_B = 256
POOL_WINDOWS = (2, 4, 8, 16)
N_POOL_GROUPS = len(POOL_WINDOWS)
POOL_GROUP_DIM = D_B // N_POOL_GROUPS
POOL_STATE = max(POOL_WINDOWS) - 1
D_C = 256
N_HEADS_C = D_C // HEAD_DIM
CONV_W = 3
D_MIX = D_A + D_B + D_C
D_IN = 2 * D_A + D_B + 3 * D_C
D_FF = -(-8 * D_MODEL // (3 * 256)) * 256
ALPHA = (2 * DEPTH) ** 0.25
BETA = (8 * DEPTH) ** -0.25
LN_EPS = 1e-5

kernel_name = 'hybrid_stream_gmlp_pool_shortconv_step'


def layer_norm(x, g, b):
    xf = x.astype(jnp.float32)
    mu = jnp.mean(xf, axis=-1, keepdims=True)
    var = jnp.mean(jnp.square(xf - mu), axis=-1, keepdims=True)
    return ((xf - mu) * lax.rsqrt(var + LN_EPS) * g.astype(jnp.float32) + b.astype(jnp.float32)).astype(x.dtype)


def chunk_mlp(u, v, w_spatial, b_spatial):
    bsz, s, _ = v.shape
    pad = (-s) % MLP_CHUNK
    vp = jnp.pad(v, ((0, 0), (0, pad), (0, 0)))
    n = (s + pad) // MLP_CHUNK
    vb = vp.reshape(bsz, n, MLP_CHUNK, N_HEADS_A, HEAD_DIM)
    blk = jnp.arange(MLP_CHUNK) // CHUNK
    mask = blk[None, :] <= blk[:, None]
    ws = jnp.where(mask[None], w_spatial, jnp.zeros((), w_spatial.dtype))
    z = jnp.einsum('hij,bnjhd->bnihd', ws, vb) + b_spatial.T[None, None, :, :, None]
    z = z.reshape(bsz, n * MLP_CHUNK, D_A)[:, :s]
    return u * z


def pool_mixer(xb, pool_prefix, pos0, w_pool, pool_scale):
    bsz, s, _ = xb.shape
    padded = jnp.concatenate([pool_prefix.astype(xb.dtype), xb], axis=1)
    cs = jnp.cumsum(padded.astype(jnp.float32), axis=1)
    cs = jnp.pad(cs, ((0, 0), (1, 0), (0, 0)))
    pos = pos0 + jnp.arange(s)
    end = POOL_STATE + 1
    groups = []
    for g, w in enumerate(POOL_WINDOWS):
        c0 = g * POOL_GROUP_DIM
        c1 = c0 + POOL_GROUP_DIM
        win = cs[:, end:end + s, c0:c1] - cs[:, end - w:end - w + s, c0:c1]
        cnt = jnp.minimum(pos + 1, w).astype(jnp.float32)[None, :, None]
        groups.append(win / cnt)
    pooled = jnp.stack(groups, axis=2)
    xg = xb.reshape(bsz, s, N_POOL_GROUPS, POOL_GROUP_DIM).astype(jnp.float32)
    diff = (pooled - xg).astype(xb.dtype)
    out = jnp.einsum('bsgc,gce->bsge', diff, w_pool).reshape(bsz, s, D_B) * pool_scale
    return out, padded[:, -POOL_STATE:]


def short_conv(gb, gc, h, conv_prefix, conv_w):
    s = h.shape[1]
    z = gc * h
    padded = jnp.concatenate([conv_prefix.astype(z.dtype), z], axis=1)
    y = conv_w[0] * padded[:, 0:s]
    for k in range(1, CONV_W):
        y = y + conv_w[k] * padded[:, k:k + s]
    return gb * y, padded[:, -(CONV_W - 1):]


def trunk_layer(x, pool_prefix, conv_prefix, pos0, ln_mix_g, ln_mix_b, w_in, v_norm_g, v_norm_b,
                w_spatial, b_spatial, w_pool, pool_scale, conv_w, w_out,
                ln_ffn_g, ln_ffn_b, w_gate, w_up, w_down):
    proj = jnp.einsum('bsd,de->bse', x, w_in)
    cuts = [D_A, 2 * D_A, 2 * D_A + D_B, 2 * D_A + D_B + D_C, 2 * D_A + D_B + 2 * D_C]
    u, v, xb, gb, gc, h = jnp.split(proj, cuts, axis=-1)
    v = layer_norm(v, v_norm_g, v_norm_b)
    out_a = chunk_mlp(u, v, w_spatial, b_spatial)
    out_b, pool_state = pool_mixer(xb, pool_prefix, pos0, w_pool, pool_scale)
    out_c, conv_state = short_conv(gb, gc, h, conv_prefix, conv_w)
    mix = jnp.einsum('bse,ed->bsd', jnp.concatenate([out_a, out_b, out_c], axis=-1), w_out)
    x = layer_norm(ALPHA * x + mix, ln_mix_g, ln_mix_b)
    hid = jax.nn.silu(jnp.einsum('bsd,df->bsf', x, w_gate)) * jnp.einsum('bsd,df->bsf', x, w_up)
    ffn = jnp.einsum('bsf,fd->bsd', hid, w_down)
    x = layer_norm(ALPHA * x + ffn, ln_ffn_g, ln_ffn_b)
    return x, pool_state, conv_state, v


def setup_inputs(seed: int = 0) -> dict:
    key = jax.random.key(seed)
    ks = jax.random.split(key, 24)
    f32 = jnp.float32
    nrm = lambda k, shp, sc: jax.random.normal(k, shp, f32) * sc
    return {
        'x_prompt': nrm(ks[0], (BATCH, SEQ, D_MODEL), 1.0),
        'x_sample': nrm(ks[1], (DEC_BATCH, DEC_SEQ, D_MODEL), 1.0),
        'state_pool': nrm(ks[2], (DEPTH, DEC_BATCH, POOL_STATE, D_B), 1.0),
        'state_conv': nrm(ks[3], (DEPTH, DEC_BATCH, CONV_W - 1, D_C), 1.0),
        'ln_mix_g': 1.0 + nrm(ks[4], (DEPTH, D_MODEL), 0.02),
        'ln_mix_b': nrm(ks[5], (DEPTH, D_MODEL), 0.02),
        'w_in': nrm(ks[6], (DEPTH, D_MODEL, D_IN), D_MODEL ** -0.5),
        'v_norm_g': 1.0 + nrm(ks[7], (DEPTH, D_A), 0.02),
        'v_norm_b': nrm(ks[8], (DEPTH, D_A), 0.02),
        'w_spatial': nrm(ks[9], (DEPTH, N_HEADS_A, MLP_CHUNK, MLP_CHUNK), MLP_CHUNK ** -0.5),
        'b_spatial': 1.0 + nrm(ks[10], (DEPTH, N_HEADS_A, MLP_CHUNK), 0.02),
        'w_pool': nrm(ks[11], (DEPTH, N_POOL_GROUPS, POOL_GROUP_DIM, POOL_GROUP_DIM), POOL_GROUP_DIM ** -0.5),
        'pool_scale': 1.0 + nrm(ks[12], (DEPTH, D_B), 0.1),
        'conv_w': nrm(ks[13], (DEPTH, CONV_W, D_C), CONV_W ** -0.5),
        'w_out': nrm(ks[14], (DEPTH, D_MIX, D_MODEL), BETA * D_MIX ** -0.5),
        'ln_ffn_g': 1.0 + nrm(ks[15], (DEPTH, D_MODEL), 0.02),
        'ln_ffn_b': nrm(ks[16], (DEPTH, D_MODEL), 0.02),
        'w_gate': nrm(ks[17], (DEPTH, D_MODEL, D_FF), D_MODEL ** -0.5),
        'w_up': nrm(ks[18], (DEPTH, D_MODEL, D_FF), D_MODEL ** -0.5),
        'w_down': nrm(ks[19], (DEPTH, D_FF, D_MODEL), BETA * D_FF ** -0.5),
    }


def reference(x_prompt, x_sample, state_pool, state_conv, ln_mix_g, ln_mix_b, w_in, v_norm_g, v_norm_b,
              w_spatial, b_spatial, w_pool, pool_scale, conv_w, w_out,
              ln_ffn_g, ln_ffn_b, w_gate, w_up, w_down):
    bp = x_prompt.shape[0]
    zero_pool = jnp.zeros((bp, POOL_STATE, D_B), x_prompt.dtype)
    zero_conv = jnp.zeros((bp, CONV_W - 1, D_C), x_prompt.dtype)
    yp, ys = x_prompt, x_sample
    pool_p, pool_s, conv_p, conv_s, v_s = [], [], [], [], []
    for l in range(DEPTH):
        params = (ln_mix_g[l], ln_mix_b[l], w_in[l], v_norm_g[l], v_norm_b[l],
                  w_spatial[l], b_spatial[l], w_pool[l], pool_scale[l], conv_w[l], w_out[l],
                  ln_ffn_g[l], ln_ffn_b[l], w_gate[l], w_up[l], w_down[l])
        yp, sp_pool, sp_conv, _ = trunk_layer(yp, zero_pool, zero_conv, 0, *params)
        ys, ss_pool, ss_conv, ss_v = trunk_layer(ys, state_pool[l], state_conv[l], PAST_LEN, *params)
        pool_p.append(sp_pool)
        conv_p.append(sp_conv)
        pool_s.append(ss_pool)
        conv_s.append(ss_conv)
        v_s.append(ss_v)
    new_pool_prompt = jnp.stack(pool_p, axis=0)
    new_pool_sample = jnp.stack(pool_s, axis=0)
    new_conv_prompt = jnp.stack(conv_p, axis=0)
    new_conv_sample = jnp.stack(conv_s, axis=0)
    new_chunk_v_sample = jnp.stack(v_s, axis=0)
    return (yp, ys, new_pool_prompt, new_pool_sample, new_conv_prompt, new_conv_sample, new_chunk_v_sample)
```

```python
import functools

import jax
import jax.numpy as jnp
from jax import lax
from jax.experimental import pallas as pl
from jax.experimental.pallas import tpu as pltpu

D_MODEL = 1024
DEPTH = 2
PAST_LEN = 4096
CHUNK = 64
MLP_CHUNK = 128
HEAD_DIM = 64
D_A = 512
N_HEADS_A = D_A // HEAD_DIM
N_PAIRS_A = N_HEADS_A // 2
D_B = 256
POOL_WINDOWS = (2, 4, 8, 16)
POOL_GROUP_DIM = D_B // len(POOL_WINDOWS)
POOL_STATE = max(POOL_WINDOWS) - 1
D_C = 256
CONV_W = 3
D_IN = 2 * D_A + D_B + 3 * D_C
D_FF = 2816
ALPHA = (2 * DEPTH) ** 0.25
LN_EPS = 1e-5

LANES = 128
POOL_PAD = 16
CONV_PAD = 8
FF_BLOCK = 256
ROW_BLOCK = 64
TILE_ROWS = 512
VMEM_LIMIT_BYTES = 56 * 1024 * 1024

_U0, _V0, _XB0, _GB0, _GC0, _H0 = 0, D_A, 2 * D_A, 2 * D_A + D_B, 2 * D_A + D_B + D_C, 2 * D_A + D_B + 2 * D_C
_OA0, _OB0, _OC0 = 0, D_A, D_A + D_B

_F32 = jnp.float32
_BF16 = jnp.bfloat16


def _layer_norm(x, g, b):
    mu = jnp.mean(x, axis=-1, keepdims=True)
    xc = x - mu
    var = jnp.mean(xc * xc, axis=-1, keepdims=True)
    return xc * lax.rsqrt(var + LN_EPS) * g + b


def _dot(a, b):
    return jnp.dot(a, b, preferred_element_type=_F32)


def _layer_tile(x_ref, w_in_ref, ws_ref, bz_ref, vg_ref, vb_ref, wpool_ref, pscale_ref, convw_ref,
                w_out_ref, g1_ref, b1_ref, wg_ref, wu_ref, wd_ref, g2_ref, b2_ref,
                y_ref, v_out_ref, proj_ref, cat_ref, x1b_ref, hid_ref, wsm_ref,
                *, first_step, seg_len, pos0, pool_prefix, conv_prefix, store_state):
    tile_rows = x_ref.shape[0]
    n_seg = tile_rows // seg_len

    proj_ref[...] = _dot(x_ref[...].astype(_BF16), w_in_ref[...])

    @pl.when(first_step)
    def _():
        i_blk = lax.broadcasted_iota(jnp.int32, (MLP_CHUNK, 2 * MLP_CHUNK), 0) // CHUNK
        j_blk = (lax.broadcasted_iota(jnp.int32, (MLP_CHUNK, 2 * MLP_CHUNK), 1) % MLP_CHUNK) // CHUNK
        for p in range(N_PAIRS_A):
            wsm_ref[p] = jnp.where(j_blk <= i_blk, ws_ref[p], 0.0).astype(_BF16)

    rows_a = min(MLP_CHUNK, seg_len)
    low_half = lax.broadcasted_iota(jnp.int32, (rows_a, LANES), 1) < HEAD_DIM
    zero_rows = jnp.zeros((MLP_CHUNK - rows_a, LANES), _BF16)
    for c in range(tile_rows // rows_a):
        rows = pl.ds(c * rows_a, rows_a)
        v = _layer_norm(proj_ref[rows, _V0:_V0 + D_A], vg_ref[...], vb_ref[...])
        if v_out_ref is not None:
            v_out_ref[rows, :] = v
        for p in range(N_PAIRS_A):
            lanes = slice(p * LANES, (p + 1) * LANES)
            vp = v[:, lanes]
            pieces = [jnp.where(low_half, vp, 0.0).astype(_BF16)]
            if rows_a < MLP_CHUNK:
                pieces.append(zero_rows)
            pieces.append(jnp.where(low_half, 0.0, vp).astype(_BF16))
            if rows_a < MLP_CHUNK:
                pieces.append(zero_rows)
            rhs = jnp.concatenate(pieces, axis=0)
            z = _dot(wsm_ref[p, :rows_a, :], rhs) + bz_ref[:rows_a, lanes]
            u = proj_ref[rows, _U0 + p * LANES:_U0 + (p + 1) * LANES]
            cat_ref[rows, _OA0 + p * LANES:_OA0 + (p + 1) * LANES] = (u * z).astype(_BF16)

    low_half = lax.broadcasted_iota(jnp.int32, (seg_len, LANES), 1) < POOL_GROUP_DIM
    pos = pos0 + lax.broadcasted_iota(jnp.int32, (seg_len, LANES), 0)
    for s in range(n_seg):
        rows = pl.ds(s * seg_len, seg_len)
        pre = pool_prefix(s)
        for half in range(D_B // LANES):
            lanes = slice(half * LANES, (half + 1) * LANES)
            xb = proj_ref[rows, _XB0 + half * LANES:_XB0 + (half + 1) * LANES]
            padded = jnp.concatenate([pre[:, lanes], xb], axis=0)
            s2 = padded + pltpu.roll(padded, 1, 0)
            s4 = s2 + pltpu.roll(s2, 2, 0)
            if half == 0:
                win_a, win_b, w_a, w_b = s2, s4, POOL_WINDOWS[0], POOL_WINDOWS[1]
            else:
                s8 = s4 + pltpu.roll(s4, 4, 0)
                s16 = s8 + pltpu.roll(s8, 8, 0)
                win_a, win_b, w_a, w_b = s8, s16, POOL_WINDOWS[2], POOL_WINDOWS[3]
            win = jnp.where(low_half, win_a[POOL_PAD:], win_b[POOL_PAD:])
            cnt = jnp.where(low_half, jnp.minimum(pos + 1, w_a), jnp.minimum(pos + 1, w_b)).astype(_F32)
            cat_ref[rows, _OB0 + half * LANES:_OB0 + (half + 1) * LANES] = (win / cnt - xb).astype(_BF16)

        zc = proj_ref[rows, _GC0:_GC0 + D_C] * proj_ref[rows, _H0:_H0 + D_C]
        padded = jnp.concatenate([conv_prefix(s), zc], axis=0)
        y = (convw_ref[0:1, :] * pltpu.roll(padded, 2, 0)[CONV_PAD:]
             + convw_ref[1:2, :] * pltpu.roll(padded, 1, 0)[CONV_PAD:]
             + convw_ref[2:3, :] * zc)
        cat_ref[rows, _OC0:_OC0 + D_C] = (proj_ref[rows, _GB0:_GB0 + D_C] * y).astype(_BF16)
        store_state(s, proj_ref[pl.ds((s + 1) * seg_len - POOL_PAD, POOL_PAD), _XB0:_XB0 + D_B],
                    zc[seg_len - CONV_PAD:])

    cat_ref[:, _OB0:_OB0 + D_B] = (_dot(cat_ref[:, _OB0:_OB0 + D_B], wpool_ref[...]) * pscale_ref[...]).astype(_BF16)

    proj_ref[:, 0:D_MODEL] = _dot(cat_ref[...], w_out_ref[...])
    for r in range(tile_rows // ROW_BLOCK):
        rows = pl.ds(r * ROW_BLOCK, ROW_BLOCK)
        x1 = _layer_norm(ALPHA * x_ref[rows, :] + proj_ref[rows, 0:D_MODEL], g1_ref[...], b1_ref[...])
        y_ref[rows, :] = x1
        x1b_ref[rows, :] = x1.astype(_BF16)

    for c in range(D_FF // FF_BLOCK):
        cols = slice(c * FF_BLOCK, (c + 1) * FF_BLOCK)
        gate = _dot(x1b_ref[...], wg_ref[:, cols])
        up = _dot(x1b_ref[...], wu_ref[:, cols])
        hid_ref[:, cols] = (gate * (1.0 / (1.0 + jnp.exp(-gate))) * up).astype(_BF16)
    proj_ref[:, D_MODEL:2 * D_MODEL] = _dot(hid_ref[...], wd_ref[...])
    for r in range(tile_rows // ROW_BLOCK):
        rows = pl.ds(r * ROW_BLOCK, ROW_BLOCK)
        y_ref[rows, :] = _layer_norm(ALPHA * y_ref[rows, :] + proj_ref[rows, D_MODEL:2 * D_MODEL],
                                     g2_ref[...], b2_ref[...])


_N_WEIGHTS = 16


def _prompt_kernel(x_ref, *refs):
    weights = refs[:_N_WEIGHTS]
    y_ref, pool_out_ref, conv_out_ref = refs[_N_WEIGHTS:_N_WEIGHTS + 3]
    proj_ref, cat_ref, x1b_ref, hid_ref, wsm_ref, pool_carry_ref, conv_carry_ref = refs[_N_WEIGHTS + 3:]
    t = pl.program_id(1)
    tile_rows = x_ref.shape[0]

    @pl.when(t == 0)
    def _():
        pool_carry_ref[...] = jnp.zeros_like(pool_carry_ref)
        conv_carry_ref[...] = jnp.zeros_like(conv_carry_ref)

    def store_state(s, pool_tail, conv_tail):
        pool_carry_ref[...] = pool_tail
        conv_carry_ref[...] = conv_tail
        pool_out_ref[0] = pool_tail
        conv_out_ref[0] = conv_tail

    _layer_tile(x_ref, *weights, y_ref, None, proj_ref, cat_ref, x1b_ref, hid_ref, wsm_ref,
                first_step=jnp.logical_and(pl.program_id(0) == 0, t == 0),
                seg_len=tile_rows, pos0=t * tile_rows,
                pool_prefix=lambda s: pool_carry_ref[...], conv_prefix=lambda s: conv_carry_ref[...],
                store_state=store_state)


def _sample_kernel(x_ref, pool_pre_ref, conv_pre_ref, *refs, seg_len):
    weights = refs[:_N_WEIGHTS]
    y_ref, pool_out_ref, conv_out_ref, v_out_ref = refs[_N_WEIGHTS:_N_WEIGHTS + 4]
    proj_ref, cat_ref, x1b_ref, hid_ref, wsm_ref = refs[_N_WEIGHTS + 4:]

    def store_state(s, pool_tail, conv_tail):
        pool_out_ref[s] = pool_tail
        conv_out_ref[s] = conv_tail

    _layer_tile(x_ref, *weights, y_ref, v_out_ref, proj_ref, cat_ref, x1b_ref, hid_ref, wsm_ref,
                first_step=pl.program_id(0) == 0,
                seg_len=seg_len, pos0=PAST_LEN,
                pool_prefix=lambda s: pool_pre_ref[s], conv_prefix=lambda s: conv_pre_ref[s],
                store_state=store_state)


def _resident(shape):
    return pl.BlockSpec(shape, lambda *_: (0,) * len(shape), pipeline_mode=pl.Buffered(1))


def _tile_scratch(tile_rows):
    return [
        pltpu.VMEM((tile_rows, D_IN), _F32),
        pltpu.VMEM((tile_rows, D_MODEL), _BF16),
        pltpu.VMEM((tile_rows, D_MODEL), _BF16),
        pltpu.VMEM((tile_rows, D_FF), _BF16),
        pltpu.VMEM((N_PAIRS_A, MLP_CHUNK, 2 * MLP_CHUNK), _BF16),
    ]


def _prompt_layer(x, weights):
    batch, seq, _ = x.shape
    tile_rows = min(TILE_ROWS, seq)
    n_tiles = seq // tile_rows
    assert seq % tile_rows == 0 and tile_rows % MLP_CHUNK == 0
    y, pool_state, conv_state = pl.pallas_call(
        _prompt_kernel,
        grid=(batch, n_tiles),
        in_specs=[pl.BlockSpec((tile_rows, D_MODEL), lambda b, t: (b * n_tiles + t, 0))]
                 + [_resident(w.shape) for w in weights],
        out_specs=[pl.BlockSpec((tile_rows, D_MODEL), lambda b, t: (b * n_tiles + t, 0)),
                   pl.BlockSpec((1, POOL_PAD, D_B), lambda b, t: (b, 0, 0)),
                   pl.BlockSpec((1, CONV_PAD, D_C), lambda b, t: (b, 0, 0))],
        out_shape=[jax.ShapeDtypeStruct((batch * seq, D_MODEL), _F32),
                   jax.ShapeDtypeStruct((batch, POOL_PAD, D_B), _F32),
                   jax.ShapeDtypeStruct((batch, CONV_PAD, D_C), _F32)],
        scratch_shapes=_tile_scratch(tile_rows) + [pltpu.VMEM((POOL_PAD, D_B), _F32),
                                                   pltpu.VMEM((CONV_PAD, D_C), _F32)],
        compiler_params=pltpu.CompilerParams(dimension_semantics=("arbitrary", "arbitrary"),
                                             vmem_limit_bytes=VMEM_LIMIT_BYTES),
        name="prompt_layer",
    )(x.reshape(batch * seq, D_MODEL), *weights)
    return y.reshape(batch, seq, D_MODEL), pool_state[:, POOL_PAD - POOL_STATE:], conv_state[:, CONV_PAD - (CONV_W - 1):]


def _sample_layer(x, pool_prefix, conv_prefix, weights):
    batch, seq, _ = x.shape
    assert seq % POOL_PAD == 0 and (seq % MLP_CHUNK == 0 or MLP_CHUNK % seq == 0 and seq % CHUNK == 0)
    seg_per_tile = max(1, min(batch, TILE_ROWS // seq))
    assert batch % seg_per_tile == 0
    tile_rows = seg_per_tile * seq
    pool_pre = jnp.pad(pool_prefix, ((0, 0), (POOL_PAD - POOL_STATE, 0), (0, 0)))
    conv_pre = jnp.pad(conv_prefix, ((0, 0), (CONV_PAD - (CONV_W - 1), 0), (0, 0)))
    y, pool_state, conv_state, v = pl.pallas_call(
        functools.partial(_sample_kernel, seg_len=seq),
        grid=(batch // seg_per_tile,),
        in_specs=[pl.BlockSpec((tile_rows, D_MODEL), lambda i: (i, 0)),
                  pl.BlockSpec((seg_per_tile, POOL_PAD, D_B), lambda i: (i, 0, 0)),
                  pl.BlockSpec((seg_per_tile, CONV_PAD, D_C), lambda i: (i, 0, 0))]
                 + [_resident(w.shape) for w in weights],
        out_specs=[pl.BlockSpec((tile_rows, D_MODEL), lambda i: (i, 0)),
                   pl.BlockSpec((seg_per_tile, POOL_PAD, D_B), lambda i: (i, 0, 0)),
                   pl.BlockSpec((seg_per_tile, CONV_PAD, D_C), lambda i: (i, 0, 0)),
                   pl.BlockSpec((tile_rows, D_A), lambda i: (i, 0))],
        out_shape=[jax.ShapeDtypeStruct((batch * seq, D_MODEL), _F32),
                   jax.ShapeDtypeStruct((batch, POOL_PAD, D_B), _F32),
                   jax.ShapeDtypeStruct((batch, CONV_PAD, D_C), _F32),
                   jax.ShapeDtypeStruct((batch * seq, D_A), _F32)],
        scratch_shapes=_tile_scratch(tile_rows),
        compiler_params=pltpu.CompilerParams(dimension_semantics=("arbitrary",),
                                             vmem_limit_bytes=VMEM_LIMIT_BYTES),
        name="sample_layer",
    )(x.reshape(batch * seq, D_MODEL), pool_pre, conv_pre, *weights)
    return (y.reshape(batch, seq, D_MODEL), pool_state[:, POOL_PAD - POOL_STATE:],
            conv_state[:, CONV_PAD - (CONV_W - 1):], v.reshape(batch, seq, D_A))


def _layer_weights(l, ln_mix_g, ln_mix_b, w_in, v_norm_g, v_norm_b, w_spatial, b_spatial, w_pool, pool_scale,
                   conv_w, w_out, ln_ffn_g, ln_ffn_b, w_gate, w_up, w_down):
    row = lambda a: a.reshape(1, -1)
    ws = w_spatial[l].reshape(N_PAIRS_A, 2, MLP_CHUNK, MLP_CHUNK).transpose(0, 2, 1, 3)
    ws = ws.reshape(N_PAIRS_A, MLP_CHUNK, 2 * MLP_CHUNK)
    bz = jnp.repeat(b_spatial[l].T, HEAD_DIM, axis=1)
    wpool = jax.scipy.linalg.block_diag(*w_pool[l]).astype(_BF16)
    return (w_in[l].astype(_BF16), ws, bz, row(v_norm_g[l]), row(v_norm_b[l]), wpool, row(pool_scale[l]),
            conv_w[l], w_out[l].astype(_BF16), row(ln_mix_g[l]), row(ln_mix_b[l]),
            w_gate[l].astype(_BF16), w_up[l].astype(_BF16), w_down[l].astype(_BF16),
            row(ln_ffn_g[l]), row(ln_ffn_b[l]))


def kernel(x_prompt, x_sample, state_pool, state_conv, ln_mix_g, ln_mix_b, w_in, v_norm_g, v_norm_b, w_spatial, b_spatial, w_pool, pool_scale, conv_w, w_out, ln_ffn_g, ln_ffn_b, w_gate, w_up, w_down):
    yp, ys = x_prompt, x_sample
    pool_p, pool_s, conv_p, conv_s, v_s = [], [], [], [], []
    for l in range(DEPTH):
        weights = _layer_weights(l, ln_mix_g, ln_mix_b, w_in, v_norm_g, v_norm_b, w_spatial, b_spatial, w_pool,
                                 pool_scale, conv_w, w_out, ln_ffn_g, ln_ffn_b, w_gate, w_up, w_down)
        yp, pp, cp = _prompt_layer(yp, weights)
        ys, ps, cs, vs = _sample_layer(ys, state_pool[l], state_conv[l], weights)
        pool_p.append(pp)
        conv_p.append(cp)
        pool_s.append(ps)
        conv_s.append(cs)
        v_s.append(vs)
    return (yp, ys, jnp.stack(pool_p), jnp.stack(pool_s), jnp.stack(conv_p), jnp.stack(conv_s), jnp.stack(v_s))
```

```python
import functools

import jax
import jax.numpy as jnp
from jax import lax
from jax.experimental import pallas as pl
from jax.experimental.pallas import tpu as pltpu

D_MODEL = 1024
DEPTH = 2
PAST_LEN = 4096
CHUNK = 64
MLP_CHUNK = 128
HEAD_DIM = 64
D_A = 512
N_HEADS_A = D_A // HEAD_DIM
N_PAIRS_A = N_HEADS_A // 2
D_B = 256
POOL_WINDOWS = (2, 4, 8, 16)
POOL_GROUP_DIM = D_B // len(POOL_WINDOWS)
POOL_STATE = max(POOL_WINDOWS) - 1
D_C = 256
CONV_W = 3
D_IN = 2 * D_A + D_B + 3 * D_C
D_FF = 2816
ALPHA = (2 * DEPTH) ** 0.25
LN_EPS = 1e-5

LANES = 128
POOL_PAD = 16
CONV_PAD = 8
FF_BLOCK = 256
ROW_BLOCK = 64
TILE_ROWS = 512
VMEM_LIMIT_BYTES = 56 * 1024 * 1024

_U0, _V0, _XB0, _GB0, _GC0, _H0 = 0, D_A, 2 * D_A, 2 * D_A + D_B, 2 * D_A + D_B + D_C, 2 * D_A + D_B + 2 * D_C
_OA0, _OB0, _OC0 = 0, D_A, D_A + D_B

_F32 = jnp.float32
_BF16 = jnp.bfloat16


def _layer_norm(x, g, b):
    mu = jnp.mean(x, axis=-1, keepdims=True)
    xc = x - mu
    var = jnp.mean(xc * xc, axis=-1, keepdims=True)
    return xc * lax.rsqrt(var + LN_EPS) * g + b


def _dot(a, b):
    return jnp.dot(a, b, preferred_element_type=_F32)


def _init_step_zero(ws_ref, wsm_ref, m_ref):
    i_blk = lax.broadcasted_iota(jnp.int32, (MLP_CHUNK, 2 * MLP_CHUNK), 0) // CHUNK
    j_blk = (lax.broadcasted_iota(jnp.int32, (MLP_CHUNK, 2 * MLP_CHUNK), 1) % MLP_CHUNK) // CHUNK
    for p in range(N_PAIRS_A):
        wsm_ref[p] = jnp.where(j_blk <= i_blk, ws_ref[p], 0.0).astype(_BF16)
    m_ref[...] = jnp.zeros_like(m_ref)


def _layer_step(x_ref, w_in_ref, ws_ref, bz_ref, vg_ref, vb_ref, wpool_ref, pscale_ref, convw_ref,
                w_out_ref, g1_ref, b1_ref, wg_ref, wu_ref, wd_ref, g2_ref, b2_ref,
                y_ref, v_out_ref, proj_ref, cat_ref, m_ref, x1b_ref, hid_ref, ffn_ref, wsm_ref,
                *, seg_len, pos0, pool_prefix, conv_prefix):
    tile_rows = x_ref.shape[0]
    n_seg = tile_rows // seg_len
    rows_a = min(MLP_CHUNK, seg_len)

    def in_proj():
        proj_ref[...] = _dot(x_ref[...].astype(_BF16), w_in_ref[...])

    def chunk_mlp(c):
        low_half = lax.broadcasted_iota(jnp.int32, (rows_a, LANES), 1) < HEAD_DIM
        zero_rows = jnp.zeros((MLP_CHUNK - rows_a, LANES), _BF16)
        rows = pl.ds(c * rows_a, rows_a)
        v = _layer_norm(proj_ref[rows, _V0:_V0 + D_A], vg_ref[...], vb_ref[...])
        if v_out_ref is not None:
            v_out_ref[rows, :] = v
        for p in range(N_PAIRS_A):
            lanes = slice(p * LANES, (p + 1) * LANES)
            vp = v[:, lanes]
            pieces = [jnp.where(low_half, vp, 0.0).astype(_BF16)]
            if rows_a < MLP_CHUNK:
                pieces.append(zero_rows)
            pieces.append(jnp.where(low_half, 0.0, vp).astype(_BF16))
            if rows_a < MLP_CHUNK:
                pieces.append(zero_rows)
            rhs = jnp.concatenate(pieces, axis=0)
            z = _dot(wsm_ref[p, :rows_a, :], rhs) + bz_ref[:rows_a, lanes]
            u = proj_ref[rows, _U0 + p * LANES:_U0 + (p + 1) * LANES]
            cat_ref[rows, _OA0 + p * LANES:_OA0 + (p + 1) * LANES] = (u * z).astype(_BF16)

    def pool_conv(s):
        low_half = lax.broadcasted_iota(jnp.int32, (seg_len, LANES), 1) < POOL_GROUP_DIM
        pos = pos0 + lax.broadcasted_iota(jnp.int32, (seg_len, LANES), 0)
        rows = pl.ds(s * seg_len, seg_len)
        pre = pool_prefix(s)
        for half in range(D_B // LANES):
            lanes = slice(half * LANES, (half + 1) * LANES)
            xb = proj_ref[rows, _XB0 + half * LANES:_XB0 + (half + 1) * LANES]
            padded = jnp.concatenate([pre[:, lanes], xb], axis=0)
            s2 = padded + pltpu.roll(padded, 1, 0)
            s4 = s2 + pltpu.roll(s2, 2, 0)
            if half == 0:
                win_a, win_b, w_a, w_b = s2, s4, POOL_WINDOWS[0], POOL_WINDOWS[1]
            else:
                s8 = s4 + pltpu.roll(s4, 4, 0)
                s16 = s8 + pltpu.roll(s8, 8, 0)
                win_a, win_b, w_a, w_b = s8, s16, POOL_WINDOWS[2], POOL_WINDOWS[3]
            win = jnp.where(low_half, win_a[POOL_PAD:], win_b[POOL_PAD:])
            cnt = jnp.where(low_half, jnp.minimum(pos + 1, w_a), jnp.minimum(pos + 1, w_b)).astype(_F32)
            cat_ref[rows, _OB0 + half * LANES:_OB0 + (half + 1) * LANES] = (win / cnt - xb).astype(_BF16)

        zc = proj_ref[rows, _GC0:_GC0 + D_C] * proj_ref[rows, _H0:_H0 + D_C]
        padded = jnp.concatenate([conv_prefix(s), zc], axis=0)
        y = (convw_ref[0:1, :] * pltpu.roll(padded, 2, 0)[CONV_PAD:]
             + convw_ref[1:2, :] * pltpu.roll(padded, 1, 0)[CONV_PAD:]
             + convw_ref[2:3, :] * zc)
        cat_ref[rows, _OC0:_OC0 + D_C] = (proj_ref[rows, _GB0:_GB0 + D_C] * y).astype(_BF16)

    def pool_map():
        mapped = _dot(cat_ref[:, _OB0:_OB0 + D_B], wpool_ref[...]) * pscale_ref[...]
        cat_ref[:, _OB0:_OB0 + D_B] = mapped.astype(_BF16)

    def out_proj():
        m_ref[...] = ALPHA * x_ref[...] + _dot(cat_ref[...], w_out_ref[...])

    def mix_norm(r):
        rows = pl.ds(r * ROW_BLOCK, ROW_BLOCK)
        x1 = _layer_norm(m_ref[rows, :], g1_ref[...], b1_ref[...])
        y_ref[rows, :] = x1
        x1b_ref[rows, :] = x1.astype(_BF16)

    def ffn_hidden(c):
        cols = slice(c * FF_BLOCK, (c + 1) * FF_BLOCK)
        gate = _dot(x1b_ref[...], wg_ref[:, cols])
        up = _dot(x1b_ref[...], wu_ref[:, cols])
        hid_ref[:, cols] = (gate * (1.0 / (1.0 + jnp.exp(-gate))) * up).astype(_BF16)

    def ffn_down():
        ffn_ref[...] = _dot(hid_ref[...], wd_ref[...])

    def ffn_norm(r):
        rows = pl.ds(r * ROW_BLOCK, ROW_BLOCK)
        y_ref[rows, :] = _layer_norm(ALPHA * y_ref[rows, :] + ffn_ref[rows, :], g2_ref[...], b2_ref[...])

    in_proj()
    for r in range(tile_rows // ROW_BLOCK):
        mix_norm(r)
    mixers = ([functools.partial(chunk_mlp, c) for c in range(tile_rows // rows_a)]
              + [functools.partial(pool_conv, s) for s in range(n_seg)] + [pool_map])
    n_ff = D_FF // FF_BLOCK
    done = 0
    for c in range(n_ff):
        ffn_hidden(c)
        upto = (len(mixers) * (c + 1)) // n_ff
        for piece in mixers[done:upto]:
            piece()
        done = upto
    ffn_down()
    out_proj()
    for r in range(tile_rows // ROW_BLOCK):
        ffn_norm(r)


def _state_tails(proj_ref, s, seg_len):
    pool_tail = proj_ref[pl.ds((s + 1) * seg_len - POOL_PAD, POOL_PAD), _XB0:_XB0 + D_B]
    rows = pl.ds((s + 1) * seg_len - CONV_PAD, CONV_PAD)
    return pool_tail, proj_ref[rows, _GC0:_GC0 + D_C] * proj_ref[rows, _H0:_H0 + D_C]


_N_WEIGHTS = 16


def _prompt_kernel(x_ref, *refs, n_tiles, n_steps):
    weights = refs[:_N_WEIGHTS]
    y_ref, pool_out_ref, conv_out_ref = refs[_N_WEIGHTS:_N_WEIGHTS + 3]
    (proj_ref, cat_ref, m_ref, x1b_ref, hid_ref, ffn_ref, wsm_ref,
     pool_carry_ref, conv_carry_ref) = refs[_N_WEIGHTS + 3:]
    step = pl.program_id(0)
    t = jnp.minimum(step, n_steps - 1) % n_tiles
    tile_rows = x_ref.shape[0]

    @pl.when(step == 0)
    def _():
        _init_step_zero(weights[1], wsm_ref, m_ref)

    @pl.when(t == 0)
    def _():
        pool_carry_ref[...] = jnp.zeros_like(pool_carry_ref)
        conv_carry_ref[...] = jnp.zeros_like(conv_carry_ref)

    _layer_step(x_ref, *weights, y_ref, None, proj_ref, cat_ref, m_ref, x1b_ref, hid_ref, ffn_ref, wsm_ref,
                seg_len=tile_rows, pos0=t * tile_rows,
                pool_prefix=lambda s: pool_carry_ref[...], conv_prefix=lambda s: conv_carry_ref[...])

    @pl.when(step < n_steps)
    def _():
        pool_tail, conv_tail = _state_tails(proj_ref, 0, tile_rows)
        pool_carry_ref[...] = pool_tail
        conv_carry_ref[...] = conv_tail
        pool_out_ref[0] = pool_tail
        conv_out_ref[0] = conv_tail


def _sample_kernel(x_ref, pool_pre_ref, conv_pre_ref, *refs, seg_len, n_steps):
    weights = refs[:_N_WEIGHTS]
    y_ref, pool_out_ref, conv_out_ref, v_out_ref = refs[_N_WEIGHTS:_N_WEIGHTS + 4]
    proj_ref, cat_ref, m_ref, x1b_ref, hid_ref, ffn_ref, wsm_ref = refs[_N_WEIGHTS + 4:]
    step = pl.program_id(0)

    @pl.when(step == 0)
    def _():
        _init_step_zero(weights[1], wsm_ref, m_ref)

    _layer_step(x_ref, *weights, y_ref, v_out_ref, proj_ref, cat_ref, m_ref, x1b_ref, hid_ref, ffn_ref, wsm_ref,
                seg_len=seg_len, pos0=PAST_LEN,
                pool_prefix=lambda s: pool_pre_ref[s], conv_prefix=lambda s: conv_pre_ref[s])

    @pl.when(step < n_steps)
    def _():
        for s in range(x_ref.shape[0] // seg_len):
            pool_out_ref[s], conv_out_ref[s] = _state_tails(proj_ref, s, seg_len)


def _resident(shape):
    return pl.BlockSpec(shape, lambda *_: (0,) * len(shape), pipeline_mode=pl.Buffered(1))


def _tile_scratch(tile_rows):
    return [
        pltpu.VMEM((tile_rows, D_IN), _F32),
        pltpu.VMEM((tile_rows, D_MODEL), _BF16),
        pltpu.VMEM((tile_rows, D_MODEL), _F32),
        pltpu.VMEM((tile_rows, D_MODEL), _BF16),
        pltpu.VMEM((tile_rows, D_FF), _BF16),
        pltpu.VMEM((tile_rows, D_MODEL), _F32),
        pltpu.VMEM((N_PAIRS_A, MLP_CHUNK, 2 * MLP_CHUNK), _BF16),
    ]


def _prompt_layer(x, weights):
    batch, seq, _ = x.shape
    tile_rows = min(TILE_ROWS, seq)
    n_tiles = seq // tile_rows
    n_steps = batch * n_tiles
    assert seq % tile_rows == 0 and tile_rows % MLP_CHUNK == 0
    mix_tile = lambda i: jnp.minimum(i, n_steps - 1)
    ffn_tile = lambda i: jnp.maximum(i - 1, 0)
    y, pool_state, conv_state = pl.pallas_call(
        functools.partial(_prompt_kernel, n_tiles=n_tiles, n_steps=n_steps),
        grid=(n_steps + 1,),
        in_specs=[pl.BlockSpec((tile_rows, D_MODEL), lambda i: (mix_tile(i), 0))]
                 + [_resident(w.shape) for w in weights],
        out_specs=[pl.BlockSpec((tile_rows, D_MODEL), lambda i: (ffn_tile(i), 0)),
                   pl.BlockSpec((1, POOL_PAD, D_B), lambda i: (mix_tile(i) // n_tiles, 0, 0)),
                   pl.BlockSpec((1, CONV_PAD, D_C), lambda i: (mix_tile(i) // n_tiles, 0, 0))],
        out_shape=[jax.ShapeDtypeStruct((batch * seq, D_MODEL), _F32),
                   jax.ShapeDtypeStruct((batch, POOL_PAD, D_B), _F32),
                   jax.ShapeDtypeStruct((batch, CONV_PAD, D_C), _F32)],
        scratch_shapes=_tile_scratch(tile_rows) + [pltpu.VMEM((POOL_PAD, D_B), _F32),
                                                   pltpu.VMEM((CONV_PAD, D_C), _F32)],
        compiler_params=pltpu.CompilerParams(dimension_semantics=("arbitrary",),
                                             vmem_limit_bytes=VMEM_LIMIT_BYTES),
        name="prompt_layer",
    )(x.reshape(batch * seq, D_MODEL), *weights)
    return y.reshape(batch, seq, D_MODEL), pool_state[:, POOL_PAD - POOL_STATE:], conv_state[:, CONV_PAD - (CONV_W - 1):]


def _sample_layer(x, pool_prefix, conv_prefix, weights):
    batch, seq, _ = x.shape
    assert seq % POOL_PAD == 0 and (seq % MLP_CHUNK == 0 or MLP_CHUNK % seq == 0 and seq % CHUNK == 0)
    seg_per_tile = max(1, min(batch, TILE_ROWS // seq))
    assert batch % seg_per_tile == 0
    tile_rows = seg_per_tile * seq
    n_steps = batch // seg_per_tile
    mix_tile = lambda i: jnp.minimum(i, n_steps - 1)
    ffn_tile = lambda i: jnp.maximum(i - 1, 0)
    pool_pre = jnp.pad(pool_prefix, ((0, 0), (POOL_PAD - POOL_STATE, 0), (0, 0)))
    conv_pre = jnp.pad(conv_prefix, ((0, 0), (CONV_PAD - (CONV_W - 1), 0), (0, 0)))
    y, pool_state, conv_state, v = pl.pallas_call(
        functools.partial(_sample_kernel, seg_len=seq, n_steps=n_steps),
        grid=(n_steps + 1,),
        in_specs=[pl.BlockSpec((tile_rows, D_MODEL), lambda i: (mix_tile(i), 0)),
                  pl.BlockSpec((seg_per_tile, POOL_PAD, D_B), lambda i: (mix_tile(i), 0, 0)),
                  pl.BlockSpec((seg_per_tile, CONV_PAD, D_C), lambda i: (mix_tile(i), 0, 0))]
                 + [_resident(w.shape) for w in weights],
        out_specs=[pl.BlockSpec((tile_rows, D_MODEL), lambda i: (ffn_tile(i), 0)),
                   pl.BlockSpec((seg_per_tile, POOL_PAD, D_B), lambda i: (mix_tile(i), 0, 0)),
                   pl.BlockSpec((seg_per_tile, CONV_PAD, D_C), lambda i: (mix_tile(i), 0, 0)),
                   pl.BlockSpec((tile_rows, D_A), lambda i: (mix_tile(i), 0))],
        out_shape=[jax.ShapeDtypeStruct((batch * seq, D_MODEL), _F32),
                   jax.ShapeDtypeStruct((batch, POOL_PAD, D_B), _F32),
                   jax.ShapeDtypeStruct((batch, CONV_PAD, D_C), _F32),
                   jax.ShapeDtypeStruct((batch * seq, D_A), _F32)],
        scratch_shapes=_tile_scratch(tile_rows),
        compiler_params=pltpu.CompilerParams(dimension_semantics=("arbitrary",),
                                             vmem_limit_bytes=VMEM_LIMIT_BYTES),
        name="sample_layer",
    )(x.reshape(batch * seq, D_MODEL), pool_pre, conv_pre, *weights)
    return (y.reshape(batch, seq, D_MODEL), pool_state[:, POOL_PAD - POOL_STATE:],
            conv_state[:, CONV_PAD - (CONV_W - 1):], v.reshape(batch, seq, D_A))


def _layer_weights(l, ln_mix_g, ln_mix_b, w_in, v_norm_g, v_norm_b, w_spatial, b_spatial, w_pool, pool_scale,
                   conv_w, w_out, ln_ffn_g, ln_ffn_b, w_gate, w_up, w_down):
    row = lambda a: a.reshape(1, -1)
    ws = w_spatial[l].reshape(N_PAIRS_A, 2, MLP_CHUNK, MLP_CHUNK).transpose(0, 2, 1, 3)
    ws = ws.reshape(N_PAIRS_A, MLP_CHUNK, 2 * MLP_CHUNK)
    bz = jnp.repeat(b_spatial[l].T, HEAD_DIM, axis=1)
    wpool = jax.scipy.linalg.block_diag(*w_pool[l]).astype(_BF16)
    return (w_in[l].astype(_BF16), ws, bz, row(v_norm_g[l]), row(v_norm_b[l]), wpool, row(pool_scale[l]),
            conv_w[l], w_out[l].astype(_BF16), row(ln_mix_g[l]), row(ln_mix_b[l]),
            w_gate[l].astype(_BF16), w_up[l].astype(_BF16), w_down[l].astype(_BF16),
            row(ln_ffn_g[l]), row(ln_ffn_b[l]))


def kernel(x_prompt, x_sample, state_pool, state_conv, ln_mix_g, ln_mix_b, w_in, v_norm_g, v_norm_b, w_spatial, b_spatial, w_pool, pool_scale, conv_w, w_out, ln_ffn_g, ln_ffn_b, w_gate, w_up, w_down):
    yp, ys = x_prompt, x_sample
    pool_p, pool_s, conv_p, conv_s, v_s = [], [], [], [], []
    for l in range(DEPTH):
        weights = _layer_weights(l, ln_mix_g, ln_mix_b, w_in, v_norm_g, v_norm_b, w_spatial, b_spatial, w_pool,
                                 pool_scale, conv_w, w_out, ln_ffn_g, ln_ffn_b, w_gate, w_up, w_down)
        yp, pp, cp = _prompt_layer(yp, weights)
        ys, ps, cs, vs = _sample_layer(ys, state_pool[l], state_conv[l], weights)
        pool_p.append(pp)
        conv_p.append(cp)
        pool_s.append(ps)
        conv_s.append(cs)
        v_s.append(vs)
    return (yp, ys, jnp.stack(pool_p), jnp.stack(pool_s), jnp.stack(conv_p), jnp.stack(conv_s), jnp.stack(v_s))
```

```python
import functools

import jax
import jax.numpy as jnp
from jax import lax
from jax.experimental import pallas as pl
from jax.experimental.pallas import tpu as pltpu

D_MODEL = 1024
DEPTH = 2
PAST_LEN = 4096
CHUNK = 64
MLP_CHUNK = 128
HEAD_DIM = 64
D_A = 512
N_HEADS_A = D_A // HEAD_DIM
N_PAIRS_A = N_HEADS_A // 2
D_B = 256
POOL_WINDOWS = (2, 4, 8, 16)
POOL_GROUP_DIM = D_B // len(POOL_WINDOWS)
POOL_STATE = max(POOL_WINDOWS) - 1
D_C = 256
CONV_W = 3
D_IN = 2 * D_A + D_B + 3 * D_C
D_FF = 2816
ALPHA = (2 * DEPTH) ** 0.25
LN_EPS = 1e-5

LANES = 128
POOL_PAD = 16
CONV_PAD = 8
FF_BLOCK = 256
ROW_BLOCK = 64
TILE_ROWS = 512
VMEM_LIMIT_BYTES = 56 * 1024 * 1024

_U0, _V0, _XB0, _GB0, _GC0, _H0 = 0, D_A, 2 * D_A, 2 * D_A + D_B, 2 * D_A + D_B + D_C, 2 * D_A + D_B + 2 * D_C
_OA0, _OB0, _OC0 = 0, D_A, D_A + D_B

_F32 = jnp.float32
_BF16 = jnp.bfloat16


def _layer_norm(x, g, b):
    mu = jnp.mean(x, axis=-1, keepdims=True)
    xc = x - mu
    var = jnp.mean(xc * xc, axis=-1, keepdims=True)
    return xc * lax.rsqrt(var + LN_EPS) * g + b


def _dot(a, b):
    return jnp.dot(a, b, preferred_element_type=_F32)


def _mask_spatial_weights(ws_ref, wsm_ref):
    i_blk = lax.broadcasted_iota(jnp.int32, (MLP_CHUNK, 2 * MLP_CHUNK), 0) // CHUNK
    j_blk = (lax.broadcasted_iota(jnp.int32, (MLP_CHUNK, 2 * MLP_CHUNK), 1) % MLP_CHUNK) // CHUNK
    for p in range(N_PAIRS_A):
        wsm_ref[p] = jnp.where(j_blk <= i_blk, ws_ref[p], 0.0).astype(_BF16)


def _layer_step(x_ref, w_in_ref, ws_ref, bz_ref, vg_ref, vb_ref, wpool_ref, pscale_ref, convw_ref,
                w_out_ref, g1_ref, b1_ref, wg_ref, wu_ref, wd_ref, g2_ref, b2_ref,
                y_ref, v_out_ref, proj_ref, cat_ref, m_ref, x1b_ref, hid_ref, ffn_ref, wsm_ref,
                *, pipelined, seg_len, pos0, pool_prefix, conv_prefix):
    tile_rows = x_ref.shape[0]
    n_seg = tile_rows // seg_len
    rows_a = min(MLP_CHUNK, seg_len)

    def in_proj():
        proj_ref[...] = _dot(x_ref[...].astype(_BF16), w_in_ref[...])

    def chunk_mlp(c):
        low_half = lax.broadcasted_iota(jnp.int32, (rows_a, LANES), 1) < HEAD_DIM
        zero_rows = jnp.zeros((MLP_CHUNK - rows_a, LANES), _BF16)
        rows = pl.ds(c * rows_a, rows_a)
        v = _layer_norm(proj_ref[rows, _V0:_V0 + D_A], vg_ref[...], vb_ref[...])
        if v_out_ref is not None:
            v_out_ref[rows, :] = v
        for p in range(N_PAIRS_A):
            lanes = slice(p * LANES, (p + 1) * LANES)
            vp = v[:, lanes]
            pieces = [jnp.where(low_half, vp, 0.0).astype(_BF16)]
            if rows_a < MLP_CHUNK:
                pieces.append(zero_rows)
            pieces.append(jnp.where(low_half, 0.0, vp).astype(_BF16))
            if rows_a < MLP_CHUNK:
                pieces.append(zero_rows)
            rhs = jnp.concatenate(pieces, axis=0)
            z = _dot(wsm_ref[p, :rows_a, :], rhs) + bz_ref[:rows_a, lanes]
            u = proj_ref[rows, _U0 + p * LANES:_U0 + (p + 1) * LANES]
            cat_ref[rows, _OA0 + p * LANES:_OA0 + (p + 1) * LANES] = (u * z).astype(_BF16)

    def pool_conv(s):
        low_half = lax.broadcasted_iota(jnp.int32, (seg_len, LANES), 1) < POOL_GROUP_DIM
        pos = pos0 + lax.broadcasted_iota(jnp.int32, (seg_len, LANES), 0)
        rows = pl.ds(s * seg_len, seg_len)
        pre = pool_prefix(s)
        for half in range(D_B // LANES):
            lanes = slice(half * LANES, (half + 1) * LANES)
            xb = proj_ref[rows, _XB0 + half * LANES:_XB0 + (half + 1) * LANES]
            padded = jnp.concatenate([pre[:, lanes], xb], axis=0)
            s2 = padded + pltpu.roll(padded, 1, 0)
            s4 = s2 + pltpu.roll(s2, 2, 0)
            if half == 0:
                win_a, win_b, w_a, w_b = s2, s4, POOL_WINDOWS[0], POOL_WINDOWS[1]
            else:
                s8 = s4 + pltpu.roll(s4, 4, 0)
                s16 = s8 + pltpu.roll(s8, 8, 0)
                win_a, win_b, w_a, w_b = s8, s16, POOL_WINDOWS[2], POOL_WINDOWS[3]
            win = jnp.where(low_half, win_a[POOL_PAD:], win_b[POOL_PAD:])
            cnt = jnp.where(low_half, jnp.minimum(pos + 1, w_a), jnp.minimum(pos + 1, w_b)).astype(_F32)
            cat_ref[rows, _OB0 + half * LANES:_OB0 + (half + 1) * LANES] = (win / cnt - xb).astype(_BF16)

        zc = proj_ref[rows, _GC0:_GC0 + D_C] * proj_ref[rows, _H0:_H0 + D_C]
        padded = jnp.concatenate([conv_prefix(s), zc], axis=0)
        y = (convw_ref[0:1, :] * pltpu.roll(padded, 2, 0)[CONV_PAD:]
             + convw_ref[1:2, :] * pltpu.roll(padded, 1, 0)[CONV_PAD:]
             + convw_ref[2:3, :] * zc)
        cat_ref[rows, _OC0:_OC0 + D_C] = (proj_ref[rows, _GB0:_GB0 + D_C] * y).astype(_BF16)

    def pool_map():
        mapped = _dot(cat_ref[:, _OB0:_OB0 + D_B], wpool_ref[...]) * pscale_ref[...]
        cat_ref[:, _OB0:_OB0 + D_B] = mapped.astype(_BF16)

    def out_proj():
        m_ref[...] = ALPHA * x_ref[...] + _dot(cat_ref[...], w_out_ref[...])

    def mix_norm(r):
        rows = pl.ds(r * ROW_BLOCK, ROW_BLOCK)
        x1 = _layer_norm(m_ref[rows, :], g1_ref[...], b1_ref[...])
        y_ref[rows, :] = x1
        x1b_ref[rows, :] = x1.astype(_BF16)

    def ffn_hidden(c):
        cols = slice(c * FF_BLOCK, (c + 1) * FF_BLOCK)
        gate = _dot(x1b_ref[...], wg_ref[:, cols])
        up = _dot(x1b_ref[...], wu_ref[:, cols])
        hid_ref[:, cols] = (gate * (1.0 / (1.0 + jnp.exp(-gate))) * up).astype(_BF16)

    def ffn_down():
        ffn_ref[...] = _dot(hid_ref[...], wd_ref[...])

    def ffn_norm(r):
        rows = pl.ds(r * ROW_BLOCK, ROW_BLOCK)
        y_ref[rows, :] = _layer_norm(ALPHA * y_ref[rows, :] + ffn_ref[rows, :], g2_ref[...], b2_ref[...])

    mixers = ([functools.partial(chunk_mlp, c) for c in range(tile_rows // rows_a)]
              + [functools.partial(pool_conv, s) for s in range(n_seg)] + [pool_map])
    mix_norms = [functools.partial(mix_norm, r) for r in range(tile_rows // ROW_BLOCK)]
    ffn_norms = [functools.partial(ffn_norm, r) for r in range(tile_rows // ROW_BLOCK)]
    n_ff = D_FF // FF_BLOCK

    if not pipelined:
        order = ([in_proj] + mixers + [out_proj] + mix_norms
                 + [functools.partial(ffn_hidden, c) for c in range(n_ff)] + [ffn_down] + ffn_norms)
        for piece in order:
            piece()
        return

    in_proj()
    for piece in mix_norms:
        piece()
    done = 0
    for c in range(n_ff):
        ffn_hidden(c)
        upto = (len(mixers) * (c + 1)) // n_ff
        for piece in mixers[done:upto]:
            piece()
        done = upto
    ffn_down()
    out_proj()
    for piece in ffn_norms:
        piece()


def _state_tails(proj_ref, s, seg_len):
    pool_tail = proj_ref[pl.ds((s + 1) * seg_len - POOL_PAD, POOL_PAD), _XB0:_XB0 + D_B]
    rows = pl.ds((s + 1) * seg_len - CONV_PAD, CONV_PAD)
    return pool_tail, proj_ref[rows, _GC0:_GC0 + D_C] * proj_ref[rows, _H0:_H0 + D_C]


_N_WEIGHTS = 16


def _prompt_kernel(x_ref, *refs, n_tiles, n_steps):
    weights = refs[:_N_WEIGHTS]
    y_ref, pool_out_ref, conv_out_ref = refs[_N_WEIGHTS:_N_WEIGHTS + 3]
    (proj_ref, cat_ref, m_ref, x1b_ref, hid_ref, ffn_ref, wsm_ref,
     pool_carry_ref, conv_carry_ref) = refs[_N_WEIGHTS + 3:]
    step = pl.program_id(0)
    t = jnp.minimum(step, n_steps - 1) % n_tiles
    tile_rows = x_ref.shape[0]

    @pl.when(step == 0)
    def _():
        _mask_spatial_weights(weights[1], wsm_ref)
        m_ref[...] = jnp.zeros_like(m_ref)

    @pl.when(t == 0)
    def _():
        pool_carry_ref[...] = jnp.zeros_like(pool_carry_ref)
        conv_carry_ref[...] = jnp.zeros_like(conv_carry_ref)

    _layer_step(x_ref, *weights, y_ref, None, proj_ref, cat_ref, m_ref, x1b_ref, hid_ref, ffn_ref, wsm_ref,
                pipelined=True, seg_len=tile_rows, pos0=t * tile_rows,
                pool_prefix=lambda s: pool_carry_ref[...], conv_prefix=lambda s: conv_carry_ref[...])

    @pl.when(step < n_steps)
    def _():
        pool_tail, conv_tail = _state_tails(proj_ref, 0, tile_rows)
        pool_carry_ref[...] = pool_tail
        conv_carry_ref[...] = conv_tail
        pool_out_ref[0] = pool_tail
        conv_out_ref[0] = conv_tail


def _sample_kernel(x_ref, pool_pre_ref, conv_pre_ref, *refs, seg_len):
    weights = refs[:_N_WEIGHTS]
    y_ref, pool_out_ref, conv_out_ref, v_out_ref = refs[_N_WEIGHTS:_N_WEIGHTS + 4]
    proj_ref, cat_ref, m_ref, x1b_ref, hid_ref, ffn_ref, wsm_ref = refs[_N_WEIGHTS + 4:]

    @pl.when(pl.program_id(0) == 0)
    def _():
        _mask_spatial_weights(weights[1], wsm_ref)

    _layer_step(x_ref, *weights, y_ref, v_out_ref, proj_ref, cat_ref, m_ref, x1b_ref, hid_ref, ffn_ref, wsm_ref,
                pipelined=False, seg_len=seg_len, pos0=PAST_LEN,
                pool_prefix=lambda s: pool_pre_ref[s], conv_prefix=lambda s: conv_pre_ref[s])

    for s in range(x_ref.shape[0] // seg_len):
        pool_out_ref[s], conv_out_ref[s] = _state_tails(proj_ref, s, seg_len)


def _resident(layer, shape):
    return pl.BlockSpec((None,) + tuple(shape[1:]), lambda *_: (layer,) + (0,) * (len(shape) - 1),
                        pipeline_mode=pl.Buffered(1))


def _tile_scratch(tile_rows):
    return [
        pltpu.VMEM((tile_rows, D_IN), _F32),
        pltpu.VMEM((tile_rows, D_MODEL), _BF16),
        pltpu.VMEM((tile_rows, D_MODEL), _F32),
        pltpu.VMEM((tile_rows, D_MODEL), _BF16),
        pltpu.VMEM((tile_rows, D_FF), _BF16),
        pltpu.VMEM((tile_rows, D_MODEL), _F32),
        pltpu.VMEM((N_PAIRS_A, MLP_CHUNK, 2 * MLP_CHUNK), _BF16),
    ]


def _prompt_layer(layer, x, weights):
    batch, seq, _ = x.shape
    tile_rows = min(TILE_ROWS, seq)
    n_tiles = seq // tile_rows
    n_steps = batch * n_tiles
    assert seq % tile_rows == 0 and tile_rows % MLP_CHUNK == 0
    mix_tile = lambda i: jnp.minimum(i, n_steps - 1)
    ffn_tile = lambda i: jnp.maximum(i - 1, 0)
    y, pool_state, conv_state = pl.pallas_call(
        functools.partial(_prompt_kernel, n_tiles=n_tiles, n_steps=n_steps),
        grid=(n_steps + 1,),
        in_specs=[pl.BlockSpec((tile_rows, D_MODEL), lambda i: (mix_tile(i), 0))]
                 + [_resident(layer, w.shape) for w in weights],
        out_specs=[pl.BlockSpec((tile_rows, D_MODEL), lambda i: (ffn_tile(i), 0)),
                   pl.BlockSpec((1, POOL_PAD, D_B), lambda i: (mix_tile(i) // n_tiles, 0, 0)),
                   pl.BlockSpec((1, CONV_PAD, D_C), lambda i: (mix_tile(i) // n_tiles, 0, 0))],
        out_shape=[jax.ShapeDtypeStruct((batch * seq, D_MODEL), _F32),
                   jax.ShapeDtypeStruct((batch, POOL_PAD, D_B), _F32),
                   jax.ShapeDtypeStruct((batch, CONV_PAD, D_C), _F32)],
        scratch_shapes=_tile_scratch(tile_rows) + [pltpu.VMEM((POOL_PAD, D_B), _F32),
                                                   pltpu.VMEM((CONV_PAD, D_C), _F32)],
        compiler_params=pltpu.CompilerParams(dimension_semantics=("arbitrary",),
                                             vmem_limit_bytes=VMEM_LIMIT_BYTES),
        name="prompt_layer",
    )(x.reshape(batch * seq, D_MODEL), *weights)
    return y.reshape(batch, seq, D_MODEL), pool_state[:, POOL_PAD - POOL_STATE:], conv_state[:, CONV_PAD - (CONV_W - 1):]


def _sample_layer(layer, x, pool_prefix, conv_prefix, weights):
    batch, seq, _ = x.shape
    assert seq % POOL_PAD == 0 and (seq % MLP_CHUNK == 0 or MLP_CHUNK % seq == 0 and seq % CHUNK == 0)
    seg_per_tile = max(1, min(batch, TILE_ROWS // seq))
    assert batch % seg_per_tile == 0
    tile_rows = seg_per_tile * seq
    pool_pre = jnp.pad(pool_prefix, ((0, 0), (POOL_PAD - POOL_STATE, 0), (0, 0)))
    conv_pre = jnp.pad(conv_prefix, ((0, 0), (CONV_PAD - (CONV_W - 1), 0), (0, 0)))
    y, pool_state, conv_state, v = pl.pallas_call(
        functools.partial(_sample_kernel, seg_len=seq),
        grid=(batch // seg_per_tile,),
        in_specs=[pl.BlockSpec((tile_rows, D_MODEL), lambda i: (i, 0)),
                  pl.BlockSpec((seg_per_tile, POOL_PAD, D_B), lambda i: (i, 0, 0)),
                  pl.BlockSpec((seg_per_tile, CONV_PAD, D_C), lambda i: (i, 0, 0))]
                 + [_resident(layer, w.shape) for w in weights],
        out_specs=[pl.BlockSpec((tile_rows, D_MODEL), lambda i: (i, 0)),
                   pl.BlockSpec((seg_per_tile, POOL_PAD, D_B), lambda i: (i, 0, 0)),
                   pl.BlockSpec((seg_per_tile, CONV_PAD, D_C), lambda i: (i, 0, 0)),
                   pl.BlockSpec((tile_rows, D_A), lambda i: (i, 0))],
        out_shape=[jax.ShapeDtypeStruct((batch * seq, D_MODEL), _F32),
                   jax.ShapeDtypeStruct((batch, POOL_PAD, D_B), _F32),
                   jax.ShapeDtypeStruct((batch, CONV_PAD, D_C), _F32),
                   jax.ShapeDtypeStruct((batch * seq, D_A), _F32)],
        scratch_shapes=_tile_scratch(tile_rows),
        compiler_params=pltpu.CompilerParams(dimension_semantics=("arbitrary",),
                                             vmem_limit_bytes=VMEM_LIMIT_BYTES),
        name="sample_layer",
    )(x.reshape(batch * seq, D_MODEL), pool_pre, conv_pre, *weights)
    return (y.reshape(batch, seq, D_MODEL), pool_state[:, POOL_PAD - POOL_STATE:],
            conv_state[:, CONV_PAD - (CONV_W - 1):], v.reshape(batch, seq, D_A))


def _stacked_weights(ln_mix_g, ln_mix_b, w_in, v_norm_g, v_norm_b, w_spatial, b_spatial, w_pool, pool_scale,
                     conv_w, w_out, ln_ffn_g, ln_ffn_b, w_gate, w_up, w_down):
    depth = w_in.shape[0]
    row = lambda a: a.reshape(depth, 1, -1)
    ws = w_spatial.reshape(depth, N_PAIRS_A, 2, MLP_CHUNK, MLP_CHUNK).transpose(0, 1, 3, 2, 4)
    ws = ws.reshape(depth, N_PAIRS_A, MLP_CHUNK, 2 * MLP_CHUNK)
    bz = jnp.repeat(b_spatial.transpose(0, 2, 1), HEAD_DIM, axis=2)
    n_groups = len(POOL_WINDOWS)
    same_group = jnp.eye(n_groups, dtype=w_pool.dtype)[None, :, None, :, None]
    wpool = (w_pool[:, :, :, None, :] * same_group).reshape(depth, D_B, D_B).astype(_BF16)
    return (w_in.astype(_BF16), ws, bz, row(v_norm_g), row(v_norm_b), wpool, row(pool_scale),
            conv_w, w_out.astype(_BF16), row(ln_mix_g), row(ln_mix_b),
            w_gate.astype(_BF16), w_up.astype(_BF16), w_down.astype(_BF16),
            row(ln_ffn_g), row(ln_ffn_b))


def kernel(x_prompt, x_sample, state_pool, state_conv, ln_mix_g, ln_mix_b, w_in, v_norm_g, v_norm_b, w_spatial, b_spatial, w_pool, pool_scale, conv_w, w_out, ln_ffn_g, ln_ffn_b, w_gate, w_up, w_down):
    weights = _stacked_weights(ln_mix_g, ln_mix_b, w_in, v_norm_g, v_norm_b, w_spatial, b_spatial, w_pool,
                               pool_scale, conv_w, w_out, ln_ffn_g, ln_ffn_b, w_gate, w_up, w_down)
    yp, ys = x_prompt, x_sample
    pool_p, pool_s, conv_p, conv_s, v_s = [], [], [], [], []
    for l in range(DEPTH):
        yp, pp, cp = _prompt_layer(l, yp, weights)
        ys, ps, cs, vs = _sample_layer(l, ys, state_pool[l], state_conv[l], weights)
        pool_p.append(pp)
        conv_p.append(cp)
        pool_s.append(ps)
        conv_s.append(cs)
        v_s.append(vs)
    return (yp, ys, jnp.stack(pool_p), jnp.stack(pool_s), jnp.stack(conv_p), jnp.stack(conv_s), jnp.stack(v_s))
```

```python
import functools

import jax
import jax.numpy as jnp
from jax import lax
from jax.experimental import pallas as pl
from jax.experimental.pallas import tpu as pltpu

D_MODEL = 1024
DEPTH = 2
PAST_LEN = 4096
CHUNK = 64
MLP_CHUNK = 128
HEAD_DIM = 64
D_A = 512
N_HEADS_A = D_A // HEAD_DIM
N_PAIRS_A = N_HEADS_A // 2
D_B = 256
POOL_WINDOWS = (2, 4, 8, 16)
POOL_GROUP_DIM = D_B // len(POOL_WINDOWS)
POOL_STATE = max(POOL_WINDOWS) - 1
D_C = 256
CONV_W = 3
D_IN = 2 * D_A + D_B + 3 * D_C
D_FF = 2816
ALPHA = (2 * DEPTH) ** 0.25
LN_EPS = 1e-5

LANES = 128
POOL_PAD = 16
CONV_PAD = 8
FF_BLOCK = 256
ROW_BLOCK = 64
DOWN_ROWS = 256
TILE_ROWS = 512
VMEM_LIMIT_BYTES = 56 * 1024 * 1024

_U0, _V0, _XB0, _GB0, _GC0, _H0 = 0, D_A, 2 * D_A, 2 * D_A + D_B, 2 * D_A + D_B + D_C, 2 * D_A + D_B + 2 * D_C
_OA0, _OB0, _OC0 = 0, D_A, D_A + D_B

_F32 = jnp.float32
_BF16 = jnp.bfloat16


def _layer_norm(x, g, b):
    mu = jnp.mean(x, axis=-1, keepdims=True)
    xc = x - mu
    var = jnp.mean(xc * xc, axis=-1, keepdims=True)
    return xc * lax.rsqrt(var + LN_EPS) * g + b


def _dot(a, b):
    return jnp.dot(a, b, preferred_element_type=_F32)


def _first_step_init(first, ws_ref, wsm_ref, *zero_refs):
    def mask_pair(p, carry):
        i_blk = lax.broadcasted_iota(jnp.int32, (MLP_CHUNK, 2 * MLP_CHUNK), 0) // CHUNK
        j_blk = (lax.broadcasted_iota(jnp.int32, (MLP_CHUNK, 2 * MLP_CHUNK), 1) % MLP_CHUNK) // CHUNK
        wsm_ref[p] = jnp.where(j_blk <= i_blk, ws_ref[p], 0.0).astype(_BF16)
        return carry

    lax.fori_loop(0, jnp.where(first, N_PAIRS_A, 0), mask_pair, 0)
    for ref in zero_refs:
        def zero_rows(i, carry, ref=ref):
            ref[pl.ds(pl.multiple_of(i * 8, 8), 8), :] = jnp.zeros((8, ref.shape[1]), ref.dtype)
            return carry

        lax.fori_loop(0, jnp.where(first, ref.shape[0] // 8, 0), zero_rows, 0)


def _layer_step(x_ref, w_in_ref, ws_ref, bz_ref, vg_ref, vb_ref, wpool_ref, pscale_ref, convw_ref,
                w_out_ref, g1_ref, b1_ref, wg_ref, wu_ref, wd_ref, g2_ref, b2_ref,
                y_ref, v_out_ref, proj_ref, cat_ref, m_ref, x1b_ref, hid_ref, ffn_ref, wsm_ref,
                *, pipelined, seg_len, pos0, pool_prefix, conv_prefix):
    tile_rows = x_ref.shape[0]
    n_seg = tile_rows // seg_len
    rows_a = min(MLP_CHUNK, seg_len)

    def in_proj():
        proj_ref[...] = _dot(x_ref[...].astype(_BF16), w_in_ref[...])

    def chunk_mlp(c):
        low_half = lax.broadcasted_iota(jnp.int32, (rows_a, LANES), 1) < HEAD_DIM
        zero_rows = jnp.zeros((MLP_CHUNK - rows_a, LANES), _BF16)
        rows = pl.ds(c * rows_a, rows_a)
        v = _layer_norm(proj_ref[rows, _V0:_V0 + D_A], vg_ref[...], vb_ref[...])
        if v_out_ref is not None:
            v_out_ref[rows, :] = v
        for p in range(N_PAIRS_A):
            lanes = slice(p * LANES, (p + 1) * LANES)
            vp = v[:, lanes]
            pieces = [jnp.where(low_half, vp, 0.0).astype(_BF16)]
            if rows_a < MLP_CHUNK:
                pieces.append(zero_rows)
            pieces.append(jnp.where(low_half, 0.0, vp).astype(_BF16))
            if rows_a < MLP_CHUNK:
                pieces.append(zero_rows)
            rhs = jnp.concatenate(pieces, axis=0)
            z = _dot(wsm_ref[p, :rows_a, :], rhs) + bz_ref[:rows_a, lanes]
            u = proj_ref[rows, _U0 + p * LANES:_U0 + (p + 1) * LANES]
            cat_ref[rows, _OA0 + p * LANES:_OA0 + (p + 1) * LANES] = (u * z).astype(_BF16)

    def pool_conv(s):
        low_half = lax.broadcasted_iota(jnp.int32, (seg_len, LANES), 1) < POOL_GROUP_DIM
        pos = pos0 + lax.broadcasted_iota(jnp.int32, (seg_len, LANES), 0)
        rows = pl.ds(s * seg_len, seg_len)
        pre = pool_prefix(s)
        for half in range(D_B // LANES):
            lanes = slice(half * LANES, (half + 1) * LANES)
            xb = proj_ref[rows, _XB0 + half * LANES:_XB0 + (half + 1) * LANES]
            padded = jnp.concatenate([pre[:, lanes], xb], axis=0)
            s2 = padded + pltpu.roll(padded, 1, 0)
            s4 = s2 + pltpu.roll(s2, 2, 0)
            if half == 0:
                win_a, win_b, w_a, w_b = s2, s4, POOL_WINDOWS[0], POOL_WINDOWS[1]
            else:
                s8 = s4 + pltpu.roll(s4, 4, 0)
                s16 = s8 + pltpu.roll(s8, 8, 0)
                win_a, win_b, w_a, w_b = s8, s16, POOL_WINDOWS[2], POOL_WINDOWS[3]
            win = jnp.where(low_half, win_a[POOL_PAD:], win_b[POOL_PAD:])
            cnt = jnp.where(low_half, jnp.minimum(pos + 1, w_a), jnp.minimum(pos + 1, w_b)).astype(_F32)
            cat_ref[rows, _OB0 + half * LANES:_OB0 + (half + 1) * LANES] = (win / cnt - xb).astype(_BF16)

        zc = proj_ref[rows, _GC0:_GC0 + D_C] * proj_ref[rows, _H0:_H0 + D_C]
        padded = jnp.concatenate([conv_prefix(s), zc], axis=0)
        y = (convw_ref[0:1, :] * pltpu.roll(padded, 2, 0)[CONV_PAD:]
             + convw_ref[1:2, :] * pltpu.roll(padded, 1, 0)[CONV_PAD:]
             + convw_ref[2:3, :] * zc)
        cat_ref[rows, _OC0:_OC0 + D_C] = (proj_ref[rows, _GB0:_GB0 + D_C] * y).astype(_BF16)

    def pool_map():
        mapped = _dot(cat_ref[:, _OB0:_OB0 + D_B], wpool_ref[...]) * pscale_ref[...]
        cat_ref[:, _OB0:_OB0 + D_B] = mapped.astype(_BF16)

    def out_proj():
        m_ref[...] = ALPHA * x_ref[...] + _dot(cat_ref[...], w_out_ref[...])

    def mix_norm(r):
        rows = pl.ds(r * ROW_BLOCK, ROW_BLOCK)
        x1 = _layer_norm(m_ref[rows, :], g1_ref[...], b1_ref[...])
        y_ref[rows, :] = x1
        x1b_ref[rows, :] = x1.astype(_BF16)

    def ffn_hidden(c):
        cols = slice(c * FF_BLOCK, (c + 1) * FF_BLOCK)
        gate = _dot(x1b_ref[...], wg_ref[:, cols])
        up = _dot(x1b_ref[...], wu_ref[:, cols])
        hid_ref[:, cols] = (gate * (1.0 / (1.0 + jnp.exp(-gate))) * up).astype(_BF16)

    def ffn_down(blk):
        rows = pl.ds(blk * DOWN_ROWS, DOWN_ROWS)
        ffn_ref[rows, :] = ALPHA * y_ref[rows, :] + _dot(hid_ref[rows, :], wd_ref[...])

    def ffn_norm(r):
        rows = pl.ds(r * ROW_BLOCK, ROW_BLOCK)
        y_ref[rows, :] = _layer_norm(ffn_ref[rows, :], g2_ref[...], b2_ref[...])

    mixers = ([functools.partial(chunk_mlp, c) for c in range(tile_rows // rows_a)]
              + [functools.partial(pool_conv, s) for s in range(n_seg)] + [pool_map])
    mix_norms = [functools.partial(mix_norm, r) for r in range(tile_rows // ROW_BLOCK)]
    ffn_norms = [functools.partial(ffn_norm, r) for r in range(tile_rows // ROW_BLOCK)]
    n_ff = D_FF // FF_BLOCK

    if not pipelined:
        order = ([in_proj] + mixers + [out_proj] + mix_norms
                 + [functools.partial(ffn_hidden, c) for c in range(n_ff)]
                 + [functools.partial(ffn_down, blk) for blk in range(tile_rows // DOWN_ROWS)] + ffn_norms)
        for piece in order:
            piece()
        return

    in_proj()
    for piece in mix_norms:
        piece()
    done = 0
    for c in range(n_ff):
        ffn_hidden(c)
        upto = (len(mixers) * (c + 1)) // n_ff
        for piece in mixers[done:upto]:
            piece()
        done = upto
    n_down = tile_rows // DOWN_ROWS
    per_block = len(ffn_norms) // n_down
    for blk in range(n_down):
        ffn_down(blk)
        if blk == n_down - 1:
            out_proj()
        for piece in ffn_norms[blk * per_block:(blk + 1) * per_block]:
            piece()


def _state_tails(proj_ref, s, seg_len):
    pool_tail = proj_ref[pl.ds((s + 1) * seg_len - POOL_PAD, POOL_PAD), _XB0:_XB0 + D_B]
    rows = pl.ds((s + 1) * seg_len - CONV_PAD, CONV_PAD)
    return pool_tail, proj_ref[rows, _GC0:_GC0 + D_C] * proj_ref[rows, _H0:_H0 + D_C]


_N_WEIGHTS = 16


def _prompt_kernel(x_ref, *refs, n_tiles, n_steps):
    weights = refs[:_N_WEIGHTS]
    y_ref, pool_out_ref, conv_out_ref = refs[_N_WEIGHTS:_N_WEIGHTS + 3]
    (proj_ref, cat_ref, m_ref, x1b_ref, hid_ref, ffn_ref, wsm_ref,
     pool_carry_ref, conv_carry_ref) = refs[_N_WEIGHTS + 3:]
    step = pl.program_id(0)
    t = jnp.minimum(step, n_steps - 1) % n_tiles
    tile_rows = x_ref.shape[0]

    _first_step_init(step == 0, weights[1], wsm_ref, m_ref)

    @pl.when(t == 0)
    def _():
        pool_carry_ref[...] = jnp.zeros_like(pool_carry_ref)
        conv_carry_ref[...] = jnp.zeros_like(conv_carry_ref)

    _layer_step(x_ref, *weights, y_ref, None, proj_ref, cat_ref, m_ref, x1b_ref, hid_ref, ffn_ref, wsm_ref,
                pipelined=True, seg_len=tile_rows, pos0=t * tile_rows,
                pool_prefix=lambda s: pool_carry_ref[...], conv_prefix=lambda s: conv_carry_ref[...])

    @pl.when(step < n_steps)
    def _():
        pool_tail, conv_tail = _state_tails(proj_ref, 0, tile_rows)
        pool_carry_ref[...] = pool_tail
        conv_carry_ref[...] = conv_tail
        pool_out_ref[0] = pool_tail
        conv_out_ref[0] = conv_tail


def _sample_kernel(x_ref, pool_pre_ref, conv_pre_ref, *refs, seg_len):
    weights = refs[:_N_WEIGHTS]
    y_ref, pool_out_ref, conv_out_ref, v_out_ref = refs[_N_WEIGHTS:_N_WEIGHTS + 4]
    proj_ref, cat_ref, m_ref, x1b_ref, hid_ref, ffn_ref, wsm_ref = refs[_N_WEIGHTS + 4:]

    _first_step_init(pl.program_id(0) == 0, weights[1], wsm_ref)

    _layer_step(x_ref, *weights, y_ref, v_out_ref, proj_ref, cat_ref, m_ref, x1b_ref, hid_ref, ffn_ref, wsm_ref,
                pipelined=False, seg_len=seg_len, pos0=PAST_LEN,
                pool_prefix=lambda s: pool_pre_ref[s], conv_prefix=lambda s: conv_pre_ref[s])

    for s in range(x_ref.shape[0] // seg_len):
        pool_out_ref[s], conv_out_ref[s] = _state_tails(proj_ref, s, seg_len)


def _resident(layer, shape):
    return pl.BlockSpec((None,) + tuple(shape[1:]), lambda *_: (layer,) + (0,) * (len(shape) - 1),
                        pipeline_mode=pl.Buffered(1))


def _tile_scratch(tile_rows):
    return [
        pltpu.VMEM((tile_rows, D_IN), _F32),
        pltpu.VMEM((tile_rows, D_MODEL), _BF16),
        pltpu.VMEM((tile_rows, D_MODEL), _F32),
        pltpu.VMEM((tile_rows, D_MODEL), _BF16),
        pltpu.VMEM((tile_rows, D_FF), _BF16),
        pltpu.VMEM((tile_rows, D_MODEL), _F32),
        pltpu.VMEM((N_PAIRS_A, MLP_CHUNK, 2 * MLP_CHUNK), _BF16),
    ]


def _prompt_layer(layer, x, weights):
    batch, seq, _ = x.shape
    tile_rows = min(TILE_ROWS, seq)
    n_tiles = seq // tile_rows
    n_steps = batch * n_tiles
    assert seq % tile_rows == 0 and tile_rows % MLP_CHUNK == 0
    mix_tile = lambda i: jnp.minimum(i, n_steps - 1)
    ffn_tile = lambda i: jnp.maximum(i - 1, 0)
    y, pool_state, conv_state = pl.pallas_call(
        functools.partial(_prompt_kernel, n_tiles=n_tiles, n_steps=n_steps),
        grid=(n_steps + 1,),
        in_specs=[pl.BlockSpec((tile_rows, D_MODEL), lambda i: (mix_tile(i), 0))]
                 + [_resident(layer, w.shape) for w in weights],
        out_specs=[pl.BlockSpec((tile_rows, D_MODEL), lambda i: (ffn_tile(i), 0)),
                   pl.BlockSpec((1, POOL_PAD, D_B), lambda i: (mix_tile(i) // n_tiles, 0, 0)),
                   pl.BlockSpec((1, CONV_PAD, D_C), lambda i: (mix_tile(i) // n_tiles, 0, 0))],
        out_shape=[jax.ShapeDtypeStruct((batch * seq, D_MODEL), _F32),
                   jax.ShapeDtypeStruct((batch, POOL_PAD, D_B), _F32),
                   jax.ShapeDtypeStruct((batch, CONV_PAD, D_C), _F32)],
        scratch_shapes=_tile_scratch(tile_rows) + [pltpu.VMEM((POOL_PAD, D_B), _F32),
                                                   pltpu.VMEM((CONV_PAD, D_C), _F32)],
        compiler_params=pltpu.CompilerParams(dimension_semantics=("arbitrary",),
                                             vmem_limit_bytes=VMEM_LIMIT_BYTES),
        name="prompt_layer",
    )(x.reshape(batch * seq, D_MODEL), *weights)
    return y.reshape(batch, seq, D_MODEL), pool_state[:, POOL_PAD - POOL_STATE:], conv_state[:, CONV_PAD - (CONV_W - 1):]


def _sample_layer(layer, x, pool_prefix, conv_prefix, weights):
    batch, seq, _ = x.shape
    assert seq % POOL_PAD == 0 and (seq % MLP_CHUNK == 0 or MLP_CHUNK % seq == 0 and seq % CHUNK == 0)
    seg_per_tile = max(1, min(batch, TILE_ROWS // seq))
    assert batch % seg_per_tile == 0
    tile_rows = seg_per_tile * seq
    pool_pre = jnp.pad(pool_prefix, ((0, 0), (POOL_PAD - POOL_STATE, 0), (0, 0)))
    conv_pre = jnp.pad(conv_prefix, ((0, 0), (CONV_PAD - (CONV_W - 1), 0), (0, 0)))
    y, pool_state, conv_state, v = pl.pallas_call(
        functools.partial(_sample_kernel, seg_len=seq),
        grid=(batch // seg_per_tile,),
        in_specs=[pl.BlockSpec((tile_rows, D_MODEL), lambda i: (i, 0)),
                  pl.BlockSpec((seg_per_tile, POOL_PAD, D_B), lambda i: (i, 0, 0)),
                  pl.BlockSpec((seg_per_tile, CONV_PAD, D_C), lambda i: (i, 0, 0))]
                 + [_resident(layer, w.shape) for w in weights],
        out_specs=[pl.BlockSpec((tile_rows, D_MODEL), lambda i: (i, 0)),
                   pl.BlockSpec((seg_per_tile, POOL_PAD, D_B), lambda i: (i, 0, 0)),
                   pl.BlockSpec((seg_per_tile, CONV_PAD, D_C), lambda i: (i, 0, 0)),
                   pl.BlockSpec((tile_rows, D_A), lambda i: (i, 0))],
        out_shape=[jax.ShapeDtypeStruct((batch * seq, D_MODEL), _F32),
                   jax.ShapeDtypeStruct((batch, POOL_PAD, D_B), _F32),
                   jax.ShapeDtypeStruct((batch, CONV_PAD, D_C), _F32),
                   jax.ShapeDtypeStruct((batch * seq, D_A), _F32)],
        scratch_shapes=_tile_scratch(tile_rows),
        compiler_params=pltpu.CompilerParams(dimension_semantics=("arbitrary",),
                                             vmem_limit_bytes=VMEM_LIMIT_BYTES),
        name="sample_layer",
    )(x.reshape(batch * seq, D_MODEL), pool_pre, conv_pre, *weights)
    return (y.reshape(batch, seq, D_MODEL), pool_state[:, POOL_PAD - POOL_STATE:],
            conv_state[:, CONV_PAD - (CONV_W - 1):], v.reshape(batch, seq, D_A))


def _stacked_weights(ln_mix_g, ln_mix_b, w_in, v_norm_g, v_norm_b, w_spatial, b_spatial, w_pool, pool_scale,
                     conv_w, w_out, ln_ffn_g, ln_ffn_b, w_gate, w_up, w_down):
    depth = w_in.shape[0]
    row = lambda a: a.reshape(depth, 1, -1)
    ws = w_spatial.reshape(depth, N_PAIRS_A, 2, MLP_CHUNK, MLP_CHUNK).transpose(0, 1, 3, 2, 4)
    ws = ws.reshape(depth, N_PAIRS_A, MLP_CHUNK, 2 * MLP_CHUNK)
    bz = jnp.repeat(b_spatial.transpose(0, 2, 1), HEAD_DIM, axis=2)
    n_groups = len(POOL_WINDOWS)
    same_group = jnp.eye(n_groups, dtype=w_pool.dtype)[None, :, None, :, None]
    wpool = (w_pool[:, :, :, None, :] * same_group).reshape(depth, D_B, D_B).astype(_BF16)
    return (w_in.astype(_BF16), ws, bz, row(v_norm_g), row(v_norm_b), wpool, row(pool_scale),
            conv_w, w_out.astype(_BF16), row(ln_mix_g), row(ln_mix_b),
            w_gate.astype(_BF16), w_up.astype(_BF16), w_down.astype(_BF16),
            row(ln_ffn_g), row(ln_ffn_b))


def kernel(x_prompt, x_sample, state_pool, state_conv, ln_mix_g, ln_mix_b, w_in, v_norm_g, v_norm_b, w_spatial, b_spatial, w_pool, pool_scale, conv_w, w_out, ln_ffn_g, ln_ffn_b, w_gate, w_up, w_down):
    weights = _stacked_weights(ln_mix_g, ln_mix_b, w_in, v_norm_g, v_norm_b, w_spatial, b_spatial, w_pool,
                               pool_scale, conv_w, w_out, ln_ffn_g, ln_ffn_b, w_gate, w_up, w_down)
    yp, ys = x_prompt, x_sample
    pool_p, pool_s, conv_p, conv_s, v_s = [], [], [], [], []
    for l in range(DEPTH):
        yp, pp, cp = _prompt_layer(l, yp, weights)
        ys, ps, cs, vs = _sample_layer(l, ys, state_pool[l], state_conv[l], weights)
        pool_p.append(pp)
        conv_p.append(cp)
        pool_s.append(ps)
        conv_s.append(cs)
        v_s.append(vs)
    return (yp, ys, jnp.stack(pool_p), jnp.stack(pool_s), jnp.stack(conv_p), jnp.stack(conv_s), jnp.stack(v_s))
```

```python
import functools

import jax
import jax.numpy as jnp
from jax import lax
from jax.experimental import pallas as pl
from jax.experimental.pallas import tpu as pltpu

D_MODEL = 1024
DEPTH = 2
PAST_LEN = 4096
CHUNK = 64
MLP_CHUNK = 128
HEAD_DIM = 64
D_A = 512
N_HEADS_A = D_A // HEAD_DIM
N_PAIRS_A = N_HEADS_A // 2
D_B = 256
POOL_WINDOWS = (2, 4, 8, 16)
POOL_GROUP_DIM = D_B // len(POOL_WINDOWS)
POOL_STATE = max(POOL_WINDOWS) - 1
D_C = 256
CONV_W = 3
D_IN = 2 * D_A + D_B + 3 * D_C
D_FF = 2816
ALPHA = (2 * DEPTH) ** 0.25
LN_EPS = 1e-5

LANES = 128
POOL_PAD = 16
CONV_PAD = 8
FF_BLOCK = 256
ROW_BLOCK = 64
DOWN_ROWS = 256
TILE_ROWS = 512
VMEM_LIMIT_BYTES = 56 * 1024 * 1024

_U0, _V0, _XB0, _GB0, _GC0, _H0 = 0, D_A, 2 * D_A, 2 * D_A + D_B, 2 * D_A + D_B + D_C, 2 * D_A + D_B + 2 * D_C
_OA0, _OB0, _OC0 = 0, D_A, D_A + D_B

_F32 = jnp.float32
_BF16 = jnp.bfloat16


def _layer_norm(x, g, b):
    mu = jnp.mean(x, axis=-1, keepdims=True)
    xc = x - mu
    var = jnp.mean(xc * xc, axis=-1, keepdims=True)
    return xc * lax.rsqrt(var + LN_EPS) * g + b


def _dot(a, b):
    return jnp.dot(a, b, preferred_element_type=_F32)


def _first_step_init(first, ws_ref, wsm_ref, *zero_refs):
    def mask_pair(p, carry):
        i_blk = lax.broadcasted_iota(jnp.int32, (MLP_CHUNK, 2 * MLP_CHUNK), 0) // CHUNK
        j_blk = (lax.broadcasted_iota(jnp.int32, (MLP_CHUNK, 2 * MLP_CHUNK), 1) % MLP_CHUNK) // CHUNK
        wsm_ref[p] = jnp.where(j_blk <= i_blk, ws_ref[p], 0.0).astype(_BF16)
        return carry

    lax.fori_loop(0, jnp.where(first, N_PAIRS_A, 0), mask_pair, 0)
    for ref in zero_refs:
        def zero_rows(i, carry, ref=ref):
            ref[pl.ds(pl.multiple_of(i * 8, 8), 8), :] = jnp.zeros((8, ref.shape[1]), ref.dtype)
            return carry

        lax.fori_loop(0, jnp.where(first, ref.shape[0] // 8, 0), zero_rows, 0)


def _layer_step(x_ref, w_in_ref, ws_ref, bz_ref, vg_ref, vb_ref, wpool_ref, pscale_ref, convw_ref,
                w_out_ref, g1_ref, b1_ref, wg_ref, wu_ref, wd_ref, g2_ref, b2_ref,
                y_ref, v_out_ref, proj_ref, cat_ref, m_ref, x1b_ref, hid_ref, ffn_ref, wsm_ref,
                *, pipelined, seg_len, pos0, pool_prefix, conv_prefix):
    tile_rows = x_ref.shape[0]
    n_seg = tile_rows // seg_len
    rows_a = min(MLP_CHUNK, seg_len)

    def in_proj():
        proj_ref[...] = _dot(x_ref[...].astype(_BF16), w_in_ref[...])

    def chunk_mlp(c):
        low_half = lax.broadcasted_iota(jnp.int32, (rows_a, LANES), 1) < HEAD_DIM
        zero_rows = jnp.zeros((MLP_CHUNK - rows_a, LANES), _BF16)
        rows = pl.ds(c * rows_a, rows_a)
        v = _layer_norm(proj_ref[rows, _V0:_V0 + D_A], vg_ref[...], vb_ref[...])
        if v_out_ref is not None:
            v_out_ref[rows, :] = v
        for p in range(N_PAIRS_A):
            lanes = slice(p * LANES, (p + 1) * LANES)
            vp = v[:, lanes]
            pieces = [jnp.where(low_half, vp, 0.0).astype(_BF16)]
            if rows_a < MLP_CHUNK:
                pieces.append(zero_rows)
            pieces.append(jnp.where(low_half, 0.0, vp).astype(_BF16))
            if rows_a < MLP_CHUNK:
                pieces.append(zero_rows)
            rhs = jnp.concatenate(pieces, axis=0)
            z = _dot(wsm_ref[p, :rows_a, :], rhs) + bz_ref[:rows_a, lanes]
            u = proj_ref[rows, _U0 + p * LANES:_U0 + (p + 1) * LANES]
            cat_ref[rows, _OA0 + p * LANES:_OA0 + (p + 1) * LANES] = (u * z).astype(_BF16)

    def pool_conv(s):
        low_half = lax.broadcasted_iota(jnp.int32, (seg_len, LANES), 1) < POOL_GROUP_DIM
        pos = pos0 + lax.broadcasted_iota(jnp.int32, (seg_len, LANES), 0)
        rows = pl.ds(s * seg_len, seg_len)
        pre = pool_prefix(s)
        for half in range(D_B // LANES):
            lanes = slice(half * LANES, (half + 1) * LANES)
            xb = proj_ref[rows, _XB0 + half * LANES:_XB0 + (half + 1) * LANES]
            padded = jnp.concatenate([pre[:, lanes], xb], axis=0)
            s2 = padded + pltpu.roll(padded, 1, 0)
            s4 = s2 + pltpu.roll(s2, 2, 0)
            if half == 0:
                win_a, win_b, w_a, w_b = s2, s4, POOL_WINDOWS[0], POOL_WINDOWS[1]
            else:
                s8 = s4 + pltpu.roll(s4, 4, 0)
                s16 = s8 + pltpu.roll(s8, 8, 0)
                win_a, win_b, w_a, w_b = s8, s16, POOL_WINDOWS[2], POOL_WINDOWS[3]
            win = jnp.where(low_half, win_a[POOL_PAD:], win_b[POOL_PAD:])
            cnt = jnp.where(low_half, jnp.minimum(pos + 1, w_a), jnp.minimum(pos + 1, w_b)).astype(_F32)
            cat_ref[rows, _OB0 + half * LANES:_OB0 + (half + 1) * LANES] = (win / cnt - xb).astype(_BF16)

        zc = proj_ref[rows, _GC0:_GC0 + D_C] * proj_ref[rows, _H0:_H0 + D_C]
        padded = jnp.concatenate([conv_prefix(s), zc], axis=0)
        y = (convw_ref[0:1, :] * pltpu.roll(padded, 2, 0)[CONV_PAD:]
             + convw_ref[1:2, :] * pltpu.roll(padded, 1, 0)[CONV_PAD:]
             + convw_ref[2:3, :] * zc)
        cat_ref[rows, _OC0:_OC0 + D_C] = (proj_ref[rows, _GB0:_GB0 + D_C] * y).astype(_BF16)

    def pool_map():
        mapped = _dot(cat_ref[:, _OB0:_OB0 + D_B], wpool_ref[...]) * pscale_ref[...]
        cat_ref[:, _OB0:_OB0 + D_B] = mapped.astype(_BF16)

    def out_proj():
        m_ref[...] = ALPHA * x_ref[...] + _dot(cat_ref[...], w_out_ref[...])

    def mix_norm(r):
        rows = pl.ds(r * ROW_BLOCK, ROW_BLOCK)
        x1 = _layer_norm(m_ref[rows, :], g1_ref[...], b1_ref[...])
        y_ref[rows, :] = x1
        x1b_ref[rows, :] = x1.astype(_BF16)

    def ffn_hidden(c, blk=0, n_rows=None):
        rows = pl.ds(blk * (n_rows or tile_rows), n_rows or tile_rows)
        cols = slice(c * FF_BLOCK, (c + 1) * FF_BLOCK)
        gate = _dot(x1b_ref[rows, :], wg_ref[:, cols])
        up = _dot(x1b_ref[rows, :], wu_ref[:, cols])
        hid_ref[rows, cols] = (gate * (1.0 / (1.0 + jnp.exp(-gate))) * up).astype(_BF16)

    def ffn_down(blk):
        rows = pl.ds(blk * DOWN_ROWS, DOWN_ROWS)
        ffn_ref[rows, :] = ALPHA * y_ref[rows, :] + _dot(hid_ref[rows, :], wd_ref[...])

    def ffn_norm(r):
        rows = pl.ds(r * ROW_BLOCK, ROW_BLOCK)
        y_ref[rows, :] = _layer_norm(ffn_ref[rows, :], g2_ref[...], b2_ref[...])

    mixers = ([functools.partial(chunk_mlp, c) for c in range(tile_rows // rows_a)]
              + [functools.partial(pool_conv, s) for s in range(n_seg)] + [pool_map])
    mix_norms = [functools.partial(mix_norm, r) for r in range(tile_rows // ROW_BLOCK)]
    ffn_norms = [functools.partial(ffn_norm, r) for r in range(tile_rows // ROW_BLOCK)]
    n_ff = D_FF // FF_BLOCK

    if not pipelined:
        order = ([in_proj] + mixers + [out_proj] + mix_norms
                 + [functools.partial(ffn_hidden, c) for c in range(n_ff)]
                 + [functools.partial(ffn_down, blk) for blk in range(tile_rows // DOWN_ROWS)] + ffn_norms)
        for piece in order:
            piece()
        return

    n_down = tile_rows // DOWN_ROWS
    per_block = len(ffn_norms) // n_down
    in_proj()
    done = 0
    for blk in range(n_down):
        for piece in mix_norms[blk * per_block:(blk + 1) * per_block]:
            piece()
        for c in range(n_ff):
            ffn_hidden(c, blk, DOWN_ROWS)
            upto = (len(mixers) * (blk * n_ff + c + 1)) // (n_down * n_ff)
            for piece in mixers[done:upto]:
                piece()
            done = upto
    for blk in range(n_down):
        ffn_down(blk)
        if blk == n_down - 1:
            out_proj()
        for piece in ffn_norms[blk * per_block:(blk + 1) * per_block]:
            piece()


def _state_tails(proj_ref, s, seg_len):
    pool_tail = proj_ref[pl.ds((s + 1) * seg_len - POOL_PAD, POOL_PAD), _XB0:_XB0 + D_B]
    rows = pl.ds((s + 1) * seg_len - CONV_PAD, CONV_PAD)
    return pool_tail, proj_ref[rows, _GC0:_GC0 + D_C] * proj_ref[rows, _H0:_H0 + D_C]


_N_WEIGHTS = 16


def _prompt_kernel(x_ref, *refs, n_tiles, n_steps):
    weights = refs[:_N_WEIGHTS]
    y_ref, pool_out_ref, conv_out_ref = refs[_N_WEIGHTS:_N_WEIGHTS + 3]
    (proj_ref, cat_ref, m_ref, x1b_ref, hid_ref, ffn_ref, wsm_ref,
     pool_carry_ref, conv_carry_ref) = refs[_N_WEIGHTS + 3:]
    step = pl.program_id(0)
    t = jnp.minimum(step, n_steps - 1) % n_tiles
    tile_rows = x_ref.shape[0]

    _first_step_init(step == 0, weights[1], wsm_ref, m_ref)

    @pl.when(t == 0)
    def _():
        pool_carry_ref[...] = jnp.zeros_like(pool_carry_ref)
        conv_carry_ref[...] = jnp.zeros_like(conv_carry_ref)

    _layer_step(x_ref, *weights, y_ref, None, proj_ref, cat_ref, m_ref, x1b_ref, hid_ref, ffn_ref, wsm_ref,
                pipelined=True, seg_len=tile_rows, pos0=t * tile_rows,
                pool_prefix=lambda s: pool_carry_ref[...], conv_prefix=lambda s: conv_carry_ref[...])

    @pl.when(step < n_steps)
    def _():
        pool_tail, conv_tail = _state_tails(proj_ref, 0, tile_rows)
        pool_carry_ref[...] = pool_tail
        conv_carry_ref[...] = conv_tail
        pool_out_ref[0] = pool_tail
        conv_out_ref[0] = conv_tail


def _sample_kernel(x_ref, pool_pre_ref, conv_pre_ref, *refs, seg_len):
    weights = refs[:_N_WEIGHTS]
    y_ref, pool_out_ref, conv_out_ref, v_out_ref = refs[_N_WEIGHTS:_N_WEIGHTS + 4]
    proj_ref, cat_ref, m_ref, x1b_ref, hid_ref, ffn_ref, wsm_ref = refs[_N_WEIGHTS + 4:]

    _first_step_init(pl.program_id(0) == 0, weights[1], wsm_ref)

    _layer_step(x_ref, *weights, y_ref, v_out_ref, proj_ref, cat_ref, m_ref, x1b_ref, hid_ref, ffn_ref, wsm_ref,
                pipelined=False, seg_len=seg_len, pos0=PAST_LEN,
                pool_prefix=lambda s: pool_pre_ref[s], conv_prefix=lambda s: conv_pre_ref[s])

    for s in range(x_ref.shape[0] // seg_len):
        pool_out_ref[s], conv_out_ref[s] = _state_tails(proj_ref, s, seg_len)


def _resident(layer, shape):
    return pl.BlockSpec((None,) + tuple(shape[1:]), lambda *_: (layer,) + (0,) * (len(shape) - 1),
                        pipeline_mode=pl.Buffered(1))


def _tile_scratch(tile_rows):
    return [
        pltpu.VMEM((tile_rows, D_IN), _F32),
        pltpu.VMEM((tile_rows, D_MODEL), _BF16),
        pltpu.VMEM((tile_rows, D_MODEL), _F32),
        pltpu.VMEM((tile_rows, D_MODEL), _BF16),
        pltpu.VMEM((tile_rows, D_FF), _BF16),
        pltpu.VMEM((tile_rows, D_MODEL), _F32),
        pltpu.VMEM((N_PAIRS_A, MLP_CHUNK, 2 * MLP_CHUNK), _BF16),
    ]


def _prompt_layer(layer, x, weights):
    batch, seq, _ = x.shape
    tile_rows = min(TILE_ROWS, seq)
    n_tiles = seq // tile_rows
    n_steps = batch * n_tiles
    assert seq % tile_rows == 0 and tile_rows % MLP_CHUNK == 0
    mix_tile = lambda i: jnp.minimum(i, n_steps - 1)
    ffn_tile = lambda i: jnp.maximum(i - 1, 0)
    y, pool_state, conv_state = pl.pallas_call(
        functools.partial(_prompt_kernel, n_tiles=n_tiles, n_steps=n_steps),
        grid=(n_steps + 1,),
        in_specs=[pl.BlockSpec((tile_rows, D_MODEL), lambda i: (mix_tile(i), 0))]
                 + [_resident(layer, w.shape) for w in weights],
        out_specs=[pl.BlockSpec((tile_rows, D_MODEL), lambda i: (ffn_tile(i), 0)),
                   pl.BlockSpec((1, POOL_PAD, D_B), lambda i: (mix_tile(i) // n_tiles, 0, 0)),
                   pl.BlockSpec((1, CONV_PAD, D_C), lambda i: (mix_tile(i) // n_tiles, 0, 0))],
        out_shape=[jax.ShapeDtypeStruct((batch * seq, D_MODEL), _F32),
                   jax.ShapeDtypeStruct((batch, POOL_PAD, D_B), _F32),
                   jax.ShapeDtypeStruct((batch, CONV_PAD, D_C), _F32)],
        scratch_shapes=_tile_scratch(tile_rows) + [pltpu.VMEM((POOL_PAD, D_B), _F32),
                                                   pltpu.VMEM((CONV_PAD, D_C), _F32)],
        compiler_params=pltpu.CompilerParams(dimension_semantics=("arbitrary",),
                                             vmem_limit_bytes=VMEM_LIMIT_BYTES),
        name="prompt_layer",
    )(x.reshape(batch * seq, D_MODEL), *weights)
    return y.reshape(batch, seq, D_MODEL), pool_state[:, POOL_PAD - POOL_STATE:], conv_state[:, CONV_PAD - (CONV_W - 1):]


def _sample_layer(layer, x, pool_prefix, conv_prefix, weights):
    batch, seq, _ = x.shape
    assert seq % POOL_PAD == 0 and (seq % MLP_CHUNK == 0 or MLP_CHUNK % seq == 0 and seq % CHUNK == 0)
    seg_per_tile = max(1, min(batch, TILE_ROWS // seq))
    assert batch % seg_per_tile == 0
    tile_rows = seg_per_tile * seq
    pool_pre = jnp.pad(pool_prefix, ((0, 0), (POOL_PAD - POOL_STATE, 0), (0, 0)))
    conv_pre = jnp.pad(conv_prefix, ((0, 0), (CONV_PAD - (CONV_W - 1), 0), (0, 0)))
    y, pool_state, conv_state, v = pl.pallas_call(
        functools.partial(_sample_kernel, seg_len=seq),
        grid=(batch // seg_per_tile,),
        in_specs=[pl.BlockSpec((tile_rows, D_MODEL), lambda i: (i, 0)),
                  pl.BlockSpec((seg_per_tile, POOL_PAD, D_B), lambda i: (i, 0, 0)),
                  pl.BlockSpec((seg_per_tile, CONV_PAD, D_C), lambda i: (i, 0, 0))]
                 + [_resident(layer, w.shape) for w in weights],
        out_specs=[pl.BlockSpec((tile_rows, D_MODEL), lambda i: (i, 0)),
                   pl.BlockSpec((seg_per_tile, POOL_PAD, D_B), lambda i: (i, 0, 0)),
                   pl.BlockSpec((seg_per_tile, CONV_PAD, D_C), lambda i: (i, 0, 0)),
                   pl.BlockSpec((tile_rows, D_A), lambda i: (i, 0))],
        out_shape=[jax.ShapeDtypeStruct((batch * seq, D_MODEL), _F32),
                   jax.ShapeDtypeStruct((batch, POOL_PAD, D_B), _F32),
                   jax.ShapeDtypeStruct((batch, CONV_PAD, D_C), _F32),
                   jax.ShapeDtypeStruct((batch * seq, D_A), _F32)],
        scratch_shapes=_tile_scratch(tile_rows),
        compiler_params=pltpu.CompilerParams(dimension_semantics=("arbitrary",),
                                             vmem_limit_bytes=VMEM_LIMIT_BYTES),
        name="sample_layer",
    )(x.reshape(batch * seq, D_MODEL), pool_pre, conv_pre, *weights)
    return (y.reshape(batch, seq, D_MODEL), pool_state[:, POOL_PAD - POOL_STATE:],
            conv_state[:, CONV_PAD - (CONV_W - 1):], v.reshape(batch, seq, D_A))


def _stacked_weights(ln_mix_g, ln_mix_b, w_in, v_norm_g, v_norm_b, w_spatial, b_spatial, w_pool, pool_scale,
                     conv_w, w_out, ln_ffn_g, ln_ffn_b, w_gate, w_up, w_down):
    depth = w_in.shape[0]
    row = lambda a: a.reshape(depth, 1, -1)
    ws = w_spatial.reshape(depth, N_PAIRS_A, 2, MLP_CHUNK, MLP_CHUNK).transpose(0, 1, 3, 2, 4)
    ws = ws.reshape(depth, N_PAIRS_A, MLP_CHUNK, 2 * MLP_CHUNK)
    bz = jnp.repeat(b_spatial.transpose(0, 2, 1), HEAD_DIM, axis=2)
    n_groups = len(POOL_WINDOWS)
    same_group = jnp.eye(n_groups, dtype=w_pool.dtype)[None, :, None, :, None]
    wpool = (w_pool[:, :, :, None, :] * same_group).reshape(depth, D_B, D_B).astype(_BF16)
    return (w_in.astype(_BF16), ws, bz, row(v_norm_g), row(v_norm_b), wpool, row(pool_scale),
            conv_w, w_out.astype(_BF16), row(ln_mix_g), row(ln_mix_b),
            w_gate.astype(_BF16), w_up.astype(_BF16), w_down.astype(_BF16),
            row(ln_ffn_g), row(ln_ffn_b))


def kernel(x_prompt, x_sample, state_pool, state_conv, ln_mix_g, ln_mix_b, w_in, v_norm_g, v_norm_b, w_spatial, b_spatial, w_pool, pool_scale, conv_w, w_out, ln_ffn_g, ln_ffn_b, w_gate, w_up, w_down):
    weights = _stacked_weights(ln_mix_g, ln_mix_b, w_in, v_norm_g, v_norm_b, w_spatial, b_spatial, w_pool,
                               pool_scale, conv_w, w_out, ln_ffn_g, ln_ffn_b, w_gate, w_up, w_down)
    yp, ys = x_prompt, x_sample
    pool_p, pool_s, conv_p, conv_s, v_s = [], [], [], [], []
    for l in range(DEPTH):
        yp, pp, cp = _prompt_layer(l, yp, weights)
        ys, ps, cs, vs = _sample_layer(l, ys, state_pool[l], state_conv[l], weights)
        pool_p.append(pp)
        conv_p.append(cp)
        pool_s.append(ps)
        conv_s.append(cs)
        v_s.append(vs)
    return (yp, ys, jnp.stack(pool_p), jnp.stack(pool_s), jnp.stack(conv_p), jnp.stack(conv_s), jnp.stack(v_s))
```

```python
import functools

import jax
import jax.numpy as jnp
from jax import lax
from jax.experimental import pallas as pl
from jax.experimental.pallas import tpu as pltpu

D_MODEL = 1024
DEPTH = 2
PAST_LEN = 4096
CHUNK = 64
MLP_CHUNK = 128
HEAD_DIM = 64
D_A = 512
N_HEADS_A = D_A // HEAD_DIM
N_PAIRS_A = N_HEADS_A // 2
D_B = 256
POOL_WINDOWS = (2, 4, 8, 16)
POOL_GROUP_DIM = D_B // len(POOL_WINDOWS)
POOL_STATE = max(POOL_WINDOWS) - 1
D_C = 256
CONV_W = 3
D_IN = 2 * D_A + D_B + 3 * D_C
D_FF = 2816
ALPHA = (2 * DEPTH) ** 0.25
LN_EPS = 1e-5

LANES = 128
POOL_PAD = 16
CONV_PAD = 8
FF_BLOCK = 256
ROW_BLOCK = 64
DOWN_ROWS = 256
TILE_ROWS = 512
VMEM_LIMIT_BYTES = 56 * 1024 * 1024

_U0, _V0, _XB0, _GB0, _GC0, _H0 = 0, D_A, 2 * D_A, 2 * D_A + D_B, 2 * D_A + D_B + D_C, 2 * D_A + D_B + 2 * D_C
_OA0, _OB0, _OC0 = 0, D_A, D_A + D_B

_F32 = jnp.float32
_BF16 = jnp.bfloat16


def _layer_norm(x, g, b):
    mu = jnp.mean(x, axis=-1, keepdims=True)
    xc = x - mu
    var = jnp.mean(xc * xc, axis=-1, keepdims=True)
    return xc * lax.rsqrt(var + LN_EPS) * g + b


def _dot(a, b):
    return jnp.dot(a, b, preferred_element_type=_F32)


def _first_step_init(first, ws_ref, wsm_ref, *zero_refs):
    def mask_pair(p, carry):
        i_blk = lax.broadcasted_iota(jnp.int32, (MLP_CHUNK, 2 * MLP_CHUNK), 0) // CHUNK
        j_blk = (lax.broadcasted_iota(jnp.int32, (MLP_CHUNK, 2 * MLP_CHUNK), 1) % MLP_CHUNK) // CHUNK
        wsm_ref[p] = jnp.where(j_blk <= i_blk, ws_ref[p], 0.0).astype(_BF16)
        return carry

    lax.fori_loop(0, jnp.where(first, N_PAIRS_A, 0), mask_pair, 0)
    for ref in zero_refs:
        def zero_rows(i, carry, ref=ref):
            ref[pl.ds(pl.multiple_of(i * 8, 8), 8), :] = jnp.zeros((8, ref.shape[1]), ref.dtype)
            return carry

        lax.fori_loop(0, jnp.where(first, ref.shape[0] // 8, 0), zero_rows, 0)


def _layer_step(x_ref, w_in_ref, ws_ref, bz_ref, vg_ref, vb_ref, wpool_ref, pscale_ref, convw_ref,
                w_out_ref, g1_ref, b1_ref, wg_ref, wu_ref, wd_ref, g2_ref, b2_ref,
                y_ref, v_out_ref, proj_ref, cat_ref, m_ref, x1b_ref, hid_ref, ffn_ref, wsm_ref,
                *, pipelined, seg_len, pos0, pool_prefix, conv_prefix):
    tile_rows = x_ref.shape[0]
    n_seg = tile_rows // seg_len
    rows_a = min(MLP_CHUNK, seg_len)

    def in_proj():
        proj_ref[...] = _dot(x_ref[...].astype(_BF16), w_in_ref[...])

    def chunk_mlp(c):
        low_half = lax.broadcasted_iota(jnp.int32, (rows_a, LANES), 1) < HEAD_DIM
        zero_rows = jnp.zeros((MLP_CHUNK - rows_a, LANES), _BF16)
        rows = pl.ds(c * rows_a, rows_a)
        v = _layer_norm(proj_ref[rows, _V0:_V0 + D_A], vg_ref[...], vb_ref[...])
        if v_out_ref is not None:
            v_out_ref[rows, :] = v
        for p in range(N_PAIRS_A):
            lanes = slice(p * LANES, (p + 1) * LANES)
            vp = v[:, lanes]
            pieces = [jnp.where(low_half, vp, 0.0).astype(_BF16)]
            if rows_a < MLP_CHUNK:
                pieces.append(zero_rows)
            pieces.append(jnp.where(low_half, 0.0, vp).astype(_BF16))
            if rows_a < MLP_CHUNK:
                pieces.append(zero_rows)
            rhs = jnp.concatenate(pieces, axis=0)
            z = _dot(wsm_ref[p, :rows_a, :], rhs) + bz_ref[:rows_a, lanes]
            u = proj_ref[rows, _U0 + p * LANES:_U0 + (p + 1) * LANES]
            cat_ref[rows, _OA0 + p * LANES:_OA0 + (p + 1) * LANES] = (u * z).astype(_BF16)

    def pool_conv(s):
        low_half = lax.broadcasted_iota(jnp.int32, (seg_len, LANES), 1) < POOL_GROUP_DIM
        pos = pos0 + lax.broadcasted_iota(jnp.int32, (seg_len, LANES), 0)
        rows = pl.ds(s * seg_len, seg_len)
        pre = pool_prefix(s)
        for half in range(D_B // LANES):
            lanes = slice(half * LANES, (half + 1) * LANES)
            xb = proj_ref[rows, _XB0 + half * LANES:_XB0 + (half + 1) * LANES]
            padded = jnp.concatenate([pre[:, lanes], xb], axis=0)
            s2 = padded + pltpu.roll(padded, 1, 0)
            s4 = s2 + pltpu.roll(s2, 2, 0)
            if half == 0:
                win_a, win_b, w_a, w_b = s2, s4, POOL_WINDOWS[0], POOL_WINDOWS[1]
            else:
                s8 = s4 + pltpu.roll(s4, 4, 0)
                s16 = s8 + pltpu.roll(s8, 8, 0)
                win_a, win_b, w_a, w_b = s8, s16, POOL_WINDOWS[2], POOL_WINDOWS[3]
            win = jnp.where(low_half, win_a[POOL_PAD:], win_b[POOL_PAD:])
            cnt = jnp.where(low_half, jnp.minimum(pos + 1, w_a), jnp.minimum(pos + 1, w_b)).astype(_F32)
            cat_ref[rows, _OB0 + half * LANES:_OB0 + (half + 1) * LANES] = (win / cnt - xb).astype(_BF16)

        zc = proj_ref[rows, _GC0:_GC0 + D_C] * proj_ref[rows, _H0:_H0 + D_C]
        padded = jnp.concatenate([conv_prefix(s), zc], axis=0)
        y = (convw_ref[0:1, :] * pltpu.roll(padded, 2, 0)[CONV_PAD:]
             + convw_ref[1:2, :] * pltpu.roll(padded, 1, 0)[CONV_PAD:]
             + convw_ref[2:3, :] * zc)
        cat_ref[rows, _OC0:_OC0 + D_C] = (proj_ref[rows, _GB0:_GB0 + D_C] * y).astype(_BF16)

    def pool_map():
        mapped = _dot(cat_ref[:, _OB0:_OB0 + D_B], wpool_ref[...]) * pscale_ref[...]
        cat_ref[:, _OB0:_OB0 + D_B] = mapped.astype(_BF16)

    def out_proj():
        m_ref[...] = ALPHA * x_ref[...] + _dot(cat_ref[...], w_out_ref[...])

    def mix_norm(r):
        rows = pl.ds(r * ROW_BLOCK, ROW_BLOCK)
        x1 = _layer_norm(m_ref[rows, :], g1_ref[...], b1_ref[...])
        y_ref[rows, :] = x1
        x1b_ref[rows, :] = x1.astype(_BF16)

    def ffn_hidden(c, blk=0, n_rows=None):
        rows = pl.ds(blk * (n_rows or tile_rows), n_rows or tile_rows)
        cols = slice(c * FF_BLOCK, (c + 1) * FF_BLOCK)
        gate = _dot(x1b_ref[rows, :], wg_ref[:, cols])
        up = _dot(x1b_ref[rows, :], wu_ref[:, cols])
        hid_ref[rows, cols] = (gate * (1.0 / (1.0 + jnp.exp(-gate))) * up).astype(_BF16)

    def ffn_down(blk):
        rows = pl.ds(blk * DOWN_ROWS, DOWN_ROWS)
        ffn_ref[rows, :] = ALPHA * y_ref[rows, :] + _dot(hid_ref[rows, :], wd_ref[...])

    def ffn_norm(r):
        rows = pl.ds(r * ROW_BLOCK, ROW_BLOCK)
        y_ref[rows, :] = _layer_norm(ffn_ref[rows, :], g2_ref[...], b2_ref[...])

    mixers = ([functools.partial(chunk_mlp, c) for c in range(tile_rows // rows_a)]
              + [functools.partial(pool_conv, s) for s in range(n_seg)] + [pool_map])
    mix_norms = [functools.partial(mix_norm, r) for r in range(tile_rows // ROW_BLOCK)]
    ffn_norms = [functools.partial(ffn_norm, r) for r in range(tile_rows // ROW_BLOCK)]
    n_ff = D_FF // FF_BLOCK

    if not pipelined:
        order = ([in_proj] + mixers + [out_proj] + mix_norms
                 + [functools.partial(ffn_hidden, c) for c in range(n_ff)]
                 + [functools.partial(ffn_down, blk) for blk in range(tile_rows // DOWN_ROWS)] + ffn_norms)
        for piece in order:
            piece()
        return

    n_down = tile_rows // DOWN_ROWS
    per_block = len(ffn_norms) // n_down
    in_proj()
    done = 0
    for piece in mix_norms[:per_block]:
        piece()
    later_norms = list(range(per_block, len(mix_norms)))
    for blk in range(n_down):
        for c in range(n_ff):
            ffn_hidden(c, blk, DOWN_ROWS)
            if blk == 0 and c >= 1 and later_norms:
                r = later_norms.pop(0)
                bits = pltpu.bitcast(hid_ref[0:16, c * FF_BLOCK:c * FF_BLOCK + LANES], jnp.uint32)
                zero = lax.shift_right_logical(lax.shift_right_logical(bits, jnp.uint32(16)), jnp.uint32(16))
                first = (pl.ds(r * ROW_BLOCK, 8), pl.ds(0, LANES))
                m_ref[first] = pltpu.bitcast(pltpu.bitcast(m_ref[first], jnp.uint32) | zero, _F32)
                mix_norms[r]()
            upto = (len(mixers) * (blk * n_ff + c + 1)) // (n_down * n_ff)
            for piece in mixers[done:upto]:
                piece()
            done = upto
    for blk in range(n_down):
        ffn_down(blk)
        if blk == n_down - 1:
            out_proj()
        for piece in ffn_norms[blk * per_block:(blk + 1) * per_block]:
            piece()


def _state_tails(proj_ref, s, seg_len):
    pool_tail = proj_ref[pl.ds((s + 1) * seg_len - POOL_PAD, POOL_PAD), _XB0:_XB0 + D_B]
    rows = pl.ds((s + 1) * seg_len - CONV_PAD, CONV_PAD)
    return pool_tail, proj_ref[rows, _GC0:_GC0 + D_C] * proj_ref[rows, _H0:_H0 + D_C]


_N_WEIGHTS = 16


def _prompt_kernel(x_ref, *refs, n_tiles, n_steps):
    weights = refs[:_N_WEIGHTS]
    y_ref, pool_out_ref, conv_out_ref = refs[_N_WEIGHTS:_N_WEIGHTS + 3]
    (proj_ref, cat_ref, m_ref, x1b_ref, hid_ref, ffn_ref, wsm_ref,
     pool_carry_ref, conv_carry_ref) = refs[_N_WEIGHTS + 3:]
    step = pl.program_id(0)
    t = jnp.minimum(step, n_steps - 1) % n_tiles
    tile_rows = x_ref.shape[0]

    _first_step_init(step == 0, weights[1], wsm_ref, m_ref)

    @pl.when(t == 0)
    def _():
        pool_carry_ref[...] = jnp.zeros_like(pool_carry_ref)
        conv_carry_ref[...] = jnp.zeros_like(conv_carry_ref)

    _layer_step(x_ref, *weights, y_ref, None, proj_ref, cat_ref, m_ref, x1b_ref, hid_ref, ffn_ref, wsm_ref,
                pipelined=True, seg_len=tile_rows, pos0=t * tile_rows,
                pool_prefix=lambda s: pool_carry_ref[...], conv_prefix=lambda s: conv_carry_ref[...])

    @pl.when(step < n_steps)
    def _():
        pool_tail, conv_tail = _state_tails(proj_ref, 0, tile_rows)
        pool_carry_ref[...] = pool_tail
        conv_carry_ref[...] = conv_tail
        pool_out_ref[0] = pool_tail
        conv_out_ref[0] = conv_tail


def _sample_kernel(x_ref, pool_pre_ref, conv_pre_ref, *refs, seg_len):
    weights = refs[:_N_WEIGHTS]
    y_ref, pool_out_ref, conv_out_ref, v_out_ref = refs[_N_WEIGHTS:_N_WEIGHTS + 4]
    proj_ref, cat_ref, m_ref, x1b_ref, hid_ref, ffn_ref, wsm_ref = refs[_N_WEIGHTS + 4:]

    _first_step_init(pl.program_id(0) == 0, weights[1], wsm_ref)

    _layer_step(x_ref, *weights, y_ref, v_out_ref, proj_ref, cat_ref, m_ref, x1b_ref, hid_ref, ffn_ref, wsm_ref,
                pipelined=False, seg_len=seg_len, pos0=PAST_LEN,
                pool_prefix=lambda s: pool_pre_ref[s], conv_prefix=lambda s: conv_pre_ref[s])

    for s in range(x_ref.shape[0] // seg_len):
        pool_out_ref[s], conv_out_ref[s] = _state_tails(proj_ref, s, seg_len)


def _resident(layer, shape):
    return pl.BlockSpec((None,) + tuple(shape[1:]), lambda *_: (layer,) + (0,) * (len(shape) - 1),
                        pipeline_mode=pl.Buffered(1))


def _tile_scratch(tile_rows):
    return [
        pltpu.VMEM((tile_rows, D_IN), _F32),
        pltpu.VMEM((tile_rows, D_MODEL), _BF16),
        pltpu.VMEM((tile_rows, D_MODEL), _F32),
        pltpu.VMEM((tile_rows, D_MODEL), _BF16),
        pltpu.VMEM((tile_rows, D_FF), _BF16),
        pltpu.VMEM((tile_rows, D_MODEL), _F32),
        pltpu.VMEM((N_PAIRS_A, MLP_CHUNK, 2 * MLP_CHUNK), _BF16),
    ]


def _prompt_layer(layer, x, weights):
    batch, seq, _ = x.shape
    tile_rows = min(TILE_ROWS, seq)
    n_tiles = seq // tile_rows
    n_steps = batch * n_tiles
    assert seq % tile_rows == 0 and tile_rows % MLP_CHUNK == 0
    mix_tile = lambda i: jnp.minimum(i, n_steps - 1)
    ffn_tile = lambda i: jnp.maximum(i - 1, 0)
    y, pool_state, conv_state = pl.pallas_call(
        functools.partial(_prompt_kernel, n_tiles=n_tiles, n_steps=n_steps),
        grid=(n_steps + 1,),
        in_specs=[pl.BlockSpec((tile_rows, D_MODEL), lambda i: (mix_tile(i), 0))]
                 + [_resident(layer, w.shape) for w in weights],
        out_specs=[pl.BlockSpec((tile_rows, D_MODEL), lambda i: (ffn_tile(i), 0)),
                   pl.BlockSpec((1, POOL_PAD, D_B), lambda i: (mix_tile(i) // n_tiles, 0, 0)),
                   pl.BlockSpec((1, CONV_PAD, D_C), lambda i: (mix_tile(i) // n_tiles, 0, 0))],
        out_shape=[jax.ShapeDtypeStruct((batch * seq, D_MODEL), _F32),
                   jax.ShapeDtypeStruct((batch, POOL_PAD, D_B), _F32),
                   jax.ShapeDtypeStruct((batch, CONV_PAD, D_C), _F32)],
        scratch_shapes=_tile_scratch(tile_rows) + [pltpu.VMEM((POOL_PAD, D_B), _F32),
                                                   pltpu.VMEM((CONV_PAD, D_C), _F32)],
        compiler_params=pltpu.CompilerParams(dimension_semantics=("arbitrary",),
                                             vmem_limit_bytes=VMEM_LIMIT_BYTES),
        name="prompt_layer",
    )(x.reshape(batch * seq, D_MODEL), *weights)
    return y.reshape(batch, seq, D_MODEL), pool_state[:, POOL_PAD - POOL_STATE:], conv_state[:, CONV_PAD - (CONV_W - 1):]


def _sample_layer(layer, x, pool_prefix, conv_prefix, weights):
    batch, seq, _ = x.shape
    assert seq % POOL_PAD == 0 and (seq % MLP_CHUNK == 0 or MLP_CHUNK % seq == 0 and seq % CHUNK == 0)
    seg_per_tile = max(1, min(batch, TILE_ROWS // seq))
    assert batch % seg_per_tile == 0
    tile_rows = seg_per_tile * seq
    pool_pre = jnp.pad(pool_prefix, ((0, 0), (POOL_PAD - POOL_STATE, 0), (0, 0)))
    conv_pre = jnp.pad(conv_prefix, ((0, 0), (CONV_PAD - (CONV_W - 1), 0), (0, 0)))
    y, pool_state, conv_state, v = pl.pallas_call(
        functools.partial(_sample_kernel, seg_len=seq),
        grid=(batch // seg_per_tile,),
        in_specs=[pl.BlockSpec((tile_rows, D_MODEL), lambda i: (i, 0)),
                  pl.BlockSpec((seg_per_tile, POOL_PAD, D_B), lambda i: (i, 0, 0)),
                  pl.BlockSpec((seg_per_tile, CONV_PAD, D_C), lambda i: (i, 0, 0))]
                 + [_resident(layer, w.shape) for w in weights],
        out_specs=[pl.BlockSpec((tile_rows, D_MODEL), lambda i: (i, 0)),
                   pl.BlockSpec((seg_per_tile, POOL_PAD, D_B), lambda i: (i, 0, 0)),
                   pl.BlockSpec((seg_per_tile, CONV_PAD, D_C), lambda i: (i, 0, 0)),
                   pl.BlockSpec((tile_rows, D_A), lambda i: (i, 0))],
        out_shape=[jax.ShapeDtypeStruct((batch * seq, D_MODEL), _F32),
                   jax.ShapeDtypeStruct((batch, POOL_PAD, D_B), _F32),
                   jax.ShapeDtypeStruct((batch, CONV_PAD, D_C), _F32),
                   jax.ShapeDtypeStruct((batch * seq, D_A), _F32)],
        scratch_shapes=_tile_scratch(tile_rows),
        compiler_params=pltpu.CompilerParams(dimension_semantics=("arbitrary",),
                                             vmem_limit_bytes=VMEM_LIMIT_BYTES),
        name="sample_layer",
    )(x.reshape(batch * seq, D_MODEL), pool_pre, conv_pre, *weights)
    return (y.reshape(batch, seq, D_MODEL), pool_state[:, POOL_PAD - POOL_STATE:],
            conv_state[:, CONV_PAD - (CONV_W - 1):], v.reshape(batch, seq, D_A))


def _stacked_weights(ln_mix_g, ln_mix_b, w_in, v_norm_g, v_norm_b, w_spatial, b_spatial, w_pool, pool_scale,
                     conv_w, w_out, ln_ffn_g, ln_ffn_b, w_gate, w_up, w_down):
    depth = w_in.shape[0]
    row = lambda a: a.reshape(depth, 1, -1)
    ws = w_spatial.reshape(depth, N_PAIRS_A, 2, MLP_CHUNK, MLP_CHUNK).transpose(0, 1, 3, 2, 4)
    ws = ws.reshape(depth, N_PAIRS_A, MLP_CHUNK, 2 * MLP_CHUNK)
    bz = jnp.repeat(b_spatial.transpose(0, 2, 1), HEAD_DIM, axis=2)
    n_groups = len(POOL_WINDOWS)
    same_group = jnp.eye(n_groups, dtype=w_pool.dtype)[None, :, None, :, None]
    wpool = (w_pool[:, :, :, None, :] * same_group).reshape(depth, D_B, D_B).astype(_BF16)
    return (w_in.astype(_BF16), ws, bz, row(v_norm_g), row(v_norm_b), wpool, row(pool_scale),
            conv_w, w_out.astype(_BF16), row(ln_mix_g), row(ln_mix_b),
            w_gate.astype(_BF16), w_up.astype(_BF16), w_down.astype(_BF16),
            row(ln_ffn_g), row(ln_ffn_b))


def kernel(x_prompt, x_sample, state_pool, state_conv, ln_mix_g, ln_mix_b, w_in, v_norm_g, v_norm_b, w_spatial, b_spatial, w_pool, pool_scale, conv_w, w_out, ln_ffn_g, ln_ffn_b, w_gate, w_up, w_down):
    weights = _stacked_weights(ln_mix_g, ln_mix_b, w_in, v_norm_g, v_norm_b, w_spatial, b_spatial, w_pool,
                               pool_scale, conv_w, w_out, ln_ffn_g, ln_ffn_b, w_gate, w_up, w_down)
    yp, ys = x_prompt, x_sample
    pool_p, pool_s, conv_p, conv_s, v_s = [], [], [], [], []
    for l in range(DEPTH):
        yp, pp, cp = _prompt_layer(l, yp, weights)
        ys, ps, cs, vs = _sample_layer(l, ys, state_pool[l], state_conv[l], weights)
        pool_p.append(pp)
        conv_p.append(cp)
        pool_s.append(ps)
        conv_s.append(cs)
        v_s.append(vs)
    return (yp, ys, jnp.stack(pool_p), jnp.stack(pool_s), jnp.stack(conv_p), jnp.stack(conv_s), jnp.stack(v_s))
```

```python
import functools

import jax
import jax.numpy as jnp
from jax import lax
from jax.experimental import pallas as pl
from jax.experimental.pallas import tpu as pltpu

D_MODEL = 1024
DEPTH = 2
PAST_LEN = 4096
CHUNK = 64
MLP_CHUNK = 128
HEAD_DIM = 64
D_A = 512
N_HEADS_A = D_A // HEAD_DIM
N_PAIRS_A = N_HEADS_A // 2
D_B = 256
POOL_WINDOWS = (2, 4, 8, 16)
POOL_GROUP_DIM = D_B // len(POOL_WINDOWS)
POOL_STATE = max(POOL_WINDOWS) - 1
D_C = 256
CONV_W = 3
D_IN = 2 * D_A + D_B + 3 * D_C
D_FF = 2816
ALPHA = (2 * DEPTH) ** 0.25
LN_EPS = 1e-5

LANES = 128
POOL_PAD = 16
CONV_PAD = 8
FF_BLOCK = 256
ROW_BLOCK = 64
DOWN_ROWS = 256
TILE_ROWS = 512
VMEM_LIMIT_BYTES = 56 * 1024 * 1024

_U0, _V0, _XB0, _GB0, _GC0, _H0 = 0, D_A, 2 * D_A, 2 * D_A + D_B, 2 * D_A + D_B + D_C, 2 * D_A + D_B + 2 * D_C
_OA0, _OB0, _OC0 = 0, D_A, D_A + D_B

_F32 = jnp.float32
_BF16 = jnp.bfloat16


def _layer_norm(x, g, b):
    mu = jnp.mean(x, axis=-1, keepdims=True)
    xc = x - mu
    var = jnp.mean(xc * xc, axis=-1, keepdims=True)
    return xc * lax.rsqrt(var + LN_EPS) * g + b


def _dot(a, b):
    return jnp.dot(a, b, preferred_element_type=_F32)


def _first_step_init(first, ws_ref, wsm_ref, *zero_refs):
    def mask_pair(p, carry):
        i_blk = lax.broadcasted_iota(jnp.int32, (MLP_CHUNK, 2 * MLP_CHUNK), 0) // CHUNK
        j_blk = (lax.broadcasted_iota(jnp.int32, (MLP_CHUNK, 2 * MLP_CHUNK), 1) % MLP_CHUNK) // CHUNK
        wsm_ref[p] = jnp.where(j_blk <= i_blk, ws_ref[p], 0.0).astype(_BF16)
        return carry

    lax.fori_loop(0, jnp.where(first, N_PAIRS_A, 0), mask_pair, 0)
    for ref in zero_refs:
        def zero_rows(i, carry, ref=ref):
            ref[pl.ds(pl.multiple_of(i * 8, 8), 8), :] = jnp.zeros((8, ref.shape[1]), ref.dtype)
            return carry

        lax.fori_loop(0, jnp.where(first, ref.shape[0] // 8, 0), zero_rows, 0)


def _layer_step(x_ref, w_in_ref, ws_ref, bz_ref, vg_ref, vb_ref, wpool_ref, pscale_ref, convw_ref,
                w_out_ref, g1_ref, b1_ref, wg_ref, wu_ref, wd_ref, g2_ref, b2_ref,
                y_ref, v_out_ref, proj_ref, cat_ref, m_ref, x1b_ref, hid_ref, ffn_ref, wsm_ref,
                *, pipelined, seg_len, pos0, pool_prefix, conv_prefix):
    tile_rows = x_ref.shape[0]
    n_seg = tile_rows // seg_len
    rows_a = min(MLP_CHUNK, seg_len)

    def in_proj():
        proj_ref[...] = _dot(x_ref[...].astype(_BF16), w_in_ref[...])

    def chunk_mlp(c):
        low_half = lax.broadcasted_iota(jnp.int32, (rows_a, LANES), 1) < HEAD_DIM
        zero_rows = jnp.zeros((MLP_CHUNK - rows_a, LANES), _BF16)
        rows = pl.ds(c * rows_a, rows_a)
        v = _layer_norm(proj_ref[rows, _V0:_V0 + D_A], vg_ref[...], vb_ref[...])
        if v_out_ref is not None:
            v_out_ref[rows, :] = v
        for p in range(N_PAIRS_A):
            lanes = slice(p * LANES, (p + 1) * LANES)
            vp = v[:, lanes]
            pieces = [jnp.where(low_half, vp, 0.0).astype(_BF16)]
            if rows_a < MLP_CHUNK:
                pieces.append(zero_rows)
            pieces.append(jnp.where(low_half, 0.0, vp).astype(_BF16))
            if rows_a < MLP_CHUNK:
                pieces.append(zero_rows)
            rhs = jnp.concatenate(pieces, axis=0)
            z = _dot(wsm_ref[p, :rows_a, :], rhs) + bz_ref[:rows_a, lanes]
            u = proj_ref[rows, _U0 + p * LANES:_U0 + (p + 1) * LANES]
            cat_ref[rows, _OA0 + p * LANES:_OA0 + (p + 1) * LANES] = (u * z).astype(_BF16)

    def pool_conv(s):
        low_half = lax.broadcasted_iota(jnp.int32, (seg_len, LANES), 1) < POOL_GROUP_DIM
        pos = pos0 + lax.broadcasted_iota(jnp.int32, (seg_len, LANES), 0)
        rows = pl.ds(s * seg_len, seg_len)
        pre = pool_prefix(s)
        for half in range(D_B // LANES):
            lanes = slice(half * LANES, (half + 1) * LANES)
            xb = proj_ref[rows, _XB0 + half * LANES:_XB0 + (half + 1) * LANES]
            padded = jnp.concatenate([pre[:, lanes], xb], axis=0)
            s2 = padded + pltpu.roll(padded, 1, 0)
            s4 = s2 + pltpu.roll(s2, 2, 0)
            if half == 0:
                win_a, win_b, w_a, w_b = s2, s4, POOL_WINDOWS[0], POOL_WINDOWS[1]
            else:
                s8 = s4 + pltpu.roll(s4, 4, 0)
                s16 = s8 + pltpu.roll(s8, 8, 0)
                win_a, win_b, w_a, w_b = s8, s16, POOL_WINDOWS[2], POOL_WINDOWS[3]
            win = jnp.where(low_half, win_a[POOL_PAD:], win_b[POOL_PAD:])
            cnt = jnp.where(low_half, jnp.minimum(pos + 1, w_a), jnp.minimum(pos + 1, w_b)).astype(_F32)
            cat_ref[rows, _OB0 + half * LANES:_OB0 + (half + 1) * LANES] = (win / cnt - xb).astype(_BF16)

        zc = proj_ref[rows, _GC0:_GC0 + D_C] * proj_ref[rows, _H0:_H0 + D_C]
        padded = jnp.concatenate([conv_prefix(s), zc], axis=0)
        y = (convw_ref[0:1, :] * pltpu.roll(padded, 2, 0)[CONV_PAD:]
             + convw_ref[1:2, :] * pltpu.roll(padded, 1, 0)[CONV_PAD:]
             + convw_ref[2:3, :] * zc)
        cat_ref[rows, _OC0:_OC0 + D_C] = (proj_ref[rows, _GB0:_GB0 + D_C] * y).astype(_BF16)

    def pool_map():
        mapped = _dot(cat_ref[:, _OB0:_OB0 + D_B], wpool_ref[...]) * pscale_ref[...]
        cat_ref[:, _OB0:_OB0 + D_B] = mapped.astype(_BF16)

    def out_proj():
        m_ref[...] = ALPHA * x_ref[...] + _dot(cat_ref[...], w_out_ref[...])

    def mix_norm(r):
        rows = pl.ds(r * ROW_BLOCK, ROW_BLOCK)
        x1 = _layer_norm(m_ref[rows, :], g1_ref[...], b1_ref[...])
        y_ref[rows, :] = x1
        x1b_ref[rows, :] = x1.astype(_BF16)

    def ffn_hidden(c, blk=0, n_rows=None):
        rows = pl.ds(blk * (n_rows or tile_rows), n_rows or tile_rows)
        cols = slice(c * FF_BLOCK, (c + 1) * FF_BLOCK)
        gate = _dot(x1b_ref[rows, :], wg_ref[:, cols])
        up = _dot(x1b_ref[rows, :], wu_ref[:, cols])
        hid_ref[rows, cols] = (gate * (1.0 / (1.0 + jnp.exp(-gate))) * up).astype(_BF16)

    def ffn_down(blk):
        rows = pl.ds(blk * DOWN_ROWS, DOWN_ROWS)
        ffn_ref[rows, :] = ALPHA * y_ref[rows, :] + _dot(hid_ref[rows, :], wd_ref[...])

    def ffn_norm(r):
        rows = pl.ds(r * ROW_BLOCK, ROW_BLOCK)
        y_ref[rows, :] = _layer_norm(ffn_ref[rows, :], g2_ref[...], b2_ref[...])

    mixers = ([functools.partial(chunk_mlp, c) for c in range(tile_rows // rows_a)]
              + [functools.partial(pool_conv, s) for s in range(n_seg)] + [pool_map])
    mix_norms = [functools.partial(mix_norm, r) for r in range(tile_rows // ROW_BLOCK)]
    ffn_norms = [functools.partial(ffn_norm, r) for r in range(tile_rows // ROW_BLOCK)]
    n_ff = D_FF // FF_BLOCK

    if not pipelined:
        order = ([in_proj] + mixers + [out_proj] + mix_norms
                 + [functools.partial(ffn_hidden, c) for c in range(n_ff)]
                 + [functools.partial(ffn_down, blk) for blk in range(tile_rows // DOWN_ROWS)] + ffn_norms)
        for piece in order:
            piece()
        return

    n_down = tile_rows // DOWN_ROWS
    per_block = len(ffn_norms) // n_down
    in_proj()
    done = 0
    for piece in mix_norms[:per_block]:
        piece()
    later_norms = list(range(per_block, len(mix_norms)))
    for blk in range(n_down):
        for c in range(n_ff):
            ffn_hidden(c, blk, DOWN_ROWS)
            if blk == 0 and c >= 1 and later_norms:
                r = later_norms.pop(0)
                bits = pltpu.bitcast(hid_ref[0:16, c * FF_BLOCK:c * FF_BLOCK + LANES], jnp.uint32)
                zero = lax.shift_right_logical(lax.shift_right_logical(bits, jnp.uint32(16)), jnp.uint32(16))
                first = (pl.ds(r * ROW_BLOCK, 8), pl.ds(0, LANES))
                m_ref[first] = m_ref[first] + pltpu.bitcast(zero, _F32)
                mix_norms[r]()
            upto = (len(mixers) * (blk * n_ff + c + 1)) // (n_down * n_ff)
            for piece in mixers[done:upto]:
                piece()
            done = upto
    for blk in range(n_down):
        ffn_down(blk)
        if blk == n_down - 1:
            out_proj()
        for piece in ffn_norms[blk * per_block:(blk + 1) * per_block]:
            piece()


def _state_tails(proj_ref, s, seg_len):
    pool_tail = proj_ref[pl.ds((s + 1) * seg_len - POOL_PAD, POOL_PAD), _XB0:_XB0 + D_B]
    rows = pl.ds((s + 1) * seg_len - CONV_PAD, CONV_PAD)
    return pool_tail, proj_ref[rows, _GC0:_GC0 + D_C] * proj_ref[rows, _H0:_H0 + D_C]


_N_WEIGHTS = 16


def _prompt_kernel(x_ref, *refs, n_tiles, n_steps):
    weights = refs[:_N_WEIGHTS]
    y_ref, pool_out_ref, conv_out_ref = refs[_N_WEIGHTS:_N_WEIGHTS + 3]
    (proj_ref, cat_ref, m_ref, x1b_ref, hid_ref, ffn_ref, wsm_ref,
     pool_carry_ref, conv_carry_ref) = refs[_N_WEIGHTS + 3:]
    step = pl.program_id(0)
    t = jnp.minimum(step, n_steps - 1) % n_tiles
    tile_rows = x_ref.shape[0]

    _first_step_init(step == 0, weights[1], wsm_ref, m_ref)

    @pl.when(t == 0)
    def _():
        pool_carry_ref[...] = jnp.zeros_like(pool_carry_ref)
        conv_carry_ref[...] = jnp.zeros_like(conv_carry_ref)

    _layer_step(x_ref, *weights, y_ref, None, proj_ref, cat_ref, m_ref, x1b_ref, hid_ref, ffn_ref, wsm_ref,
                pipelined=True, seg_len=tile_rows, pos0=t * tile_rows,
                pool_prefix=lambda s: pool_carry_ref[...], conv_prefix=lambda s: conv_carry_ref[...])

    @pl.when(step < n_steps)
    def _():
        pool_tail, conv_tail = _state_tails(proj_ref, 0, tile_rows)
        pool_carry_ref[...] = pool_tail
        conv_carry_ref[...] = conv_tail
        pool_out_ref[0] = pool_tail
        conv_out_ref[0] = conv_tail


def _sample_kernel(x_ref, pool_pre_ref, conv_pre_ref, *refs, seg_len):
    weights = refs[:_N_WEIGHTS]
    y_ref, pool_out_ref, conv_out_ref, v_out_ref = refs[_N_WEIGHTS:_N_WEIGHTS + 4]
    proj_ref, cat_ref, m_ref, x1b_ref, hid_ref, ffn_ref, wsm_ref = refs[_N_WEIGHTS + 4:]

    _first_step_init(pl.program_id(0) == 0, weights[1], wsm_ref)

    _layer_step(x_ref, *weights, y_ref, v_out_ref, proj_ref, cat_ref, m_ref, x1b_ref, hid_ref, ffn_ref, wsm_ref,
                pipelined=False, seg_len=seg_len, pos0=PAST_LEN,
                pool_prefix=lambda s: pool_pre_ref[s], conv_prefix=lambda s: conv_pre_ref[s])

    for s in range(x_ref.shape[0] // seg_len):
        pool_out_ref[s], conv_out_ref[s] = _state_tails(proj_ref, s, seg_len)


def _resident(layer, shape):
    return pl.BlockSpec((None,) + tuple(shape[1:]), lambda *_: (layer,) + (0,) * (len(shape) - 1),
                        pipeline_mode=pl.Buffered(1))


def _tile_scratch(tile_rows):
    return [
        pltpu.VMEM((tile_rows, D_IN), _F32),
        pltpu.VMEM((tile_rows, D_MODEL), _BF16),
        pltpu.VMEM((tile_rows, D_MODEL), _F32),
        pltpu.VMEM((tile_rows, D_MODEL), _BF16),
        pltpu.VMEM((tile_rows, D_FF), _BF16),
        pltpu.VMEM((tile_rows, D_MODEL), _F32),
        pltpu.VMEM((N_PAIRS_A, MLP_CHUNK, 2 * MLP_CHUNK), _BF16),
    ]


def _prompt_layer(layer, x, weights):
    batch, seq, _ = x.shape
    tile_rows = min(TILE_ROWS, seq)
    n_tiles = seq // tile_rows
    n_steps = batch * n_tiles
    assert seq % tile_rows == 0 and tile_rows % MLP_CHUNK == 0
    mix_tile = lambda i: jnp.minimum(i, n_steps - 1)
    ffn_tile = lambda i: jnp.maximum(i - 1, 0)
    y, pool_state, conv_state = pl.pallas_call(
        functools.partial(_prompt_kernel, n_tiles=n_tiles, n_steps=n_steps),
        grid=(n_steps + 1,),
        in_specs=[pl.BlockSpec((tile_rows, D_MODEL), lambda i: (mix_tile(i), 0))]
                 + [_resident(layer, w.shape) for w in weights],
        out_specs=[pl.BlockSpec((tile_rows, D_MODEL), lambda i: (ffn_tile(i), 0)),
                   pl.BlockSpec((1, POOL_PAD, D_B), lambda i: (mix_tile(i) // n_tiles, 0, 0)),
                   pl.BlockSpec((1, CONV_PAD, D_C), lambda i: (mix_tile(i) // n_tiles, 0, 0))],
        out_shape=[jax.ShapeDtypeStruct((batch * seq, D_MODEL), _F32),
                   jax.ShapeDtypeStruct((batch, POOL_PAD, D_B), _F32),
                   jax.ShapeDtypeStruct((batch, CONV_PAD, D_C), _F32)],
        scratch_shapes=_tile_scratch(tile_rows) + [pltpu.VMEM((POOL_PAD, D_B), _F32),
                                                   pltpu.VMEM((CONV_PAD, D_C), _F32)],
        compiler_params=pltpu.CompilerParams(dimension_semantics=("arbitrary",),
                                             vmem_limit_bytes=VMEM_LIMIT_BYTES),
        name="prompt_layer",
    )(x.reshape(batch * seq, D_MODEL), *weights)
    return y.reshape(batch, seq, D_MODEL), pool_state[:, POOL_PAD - POOL_STATE:], conv_state[:, CONV_PAD - (CONV_W - 1):]


def _sample_layer(layer, x, pool_prefix, conv_prefix, weights):
    batch, seq, _ = x.shape
    assert seq % POOL_PAD == 0 and (seq % MLP_CHUNK == 0 or MLP_CHUNK % seq == 0 and seq % CHUNK == 0)
    seg_per_tile = max(1, min(batch, TILE_ROWS // seq))
    assert batch % seg_per_tile == 0
    tile_rows = seg_per_tile * seq
    pool_pre = jnp.pad(pool_prefix, ((0, 0), (POOL_PAD - POOL_STATE, 0), (0, 0)))
    conv_pre = jnp.pad(conv_prefix, ((0, 0), (CONV_PAD - (CONV_W - 1), 0), (0, 0)))
    y, pool_state, conv_state, v = pl.pallas_call(
        functools.partial(_sample_kernel, seg_len=seq),
        grid=(batch // seg_per_tile,),
        in_specs=[pl.BlockSpec((tile_rows, D_MODEL), lambda i: (i, 0)),
                  pl.BlockSpec((seg_per_tile, POOL_PAD, D_B), lambda i: (i, 0, 0)),
                  pl.BlockSpec((seg_per_tile, CONV_PAD, D_C), lambda i: (i, 0, 0))]
                 + [_resident(layer, w.shape) for w in weights],
        out_specs=[pl.BlockSpec((tile_rows, D_MODEL), lambda i: (i, 0)),
                   pl.BlockSpec((seg_per_tile, POOL_PAD, D_B), lambda i: (i, 0, 0)),
                   pl.BlockSpec((seg_per_tile, CONV_PAD, D_C), lambda i: (i, 0, 0)),
                   pl.BlockSpec((tile_rows, D_A), lambda i: (i, 0))],
        out_shape=[jax.ShapeDtypeStruct((batch * seq, D_MODEL), _F32),
                   jax.ShapeDtypeStruct((batch, POOL_PAD, D_B), _F32),
                   jax.ShapeDtypeStruct((batch, CONV_PAD, D_C), _F32),
                   jax.ShapeDtypeStruct((batch * seq, D_A), _F32)],
        scratch_shapes=_tile_scratch(tile_rows),
        compiler_params=pltpu.CompilerParams(dimension_semantics=("arbitrary",),
                                             vmem_limit_bytes=VMEM_LIMIT_BYTES),
        name="sample_layer",
    )(x.reshape(batch * seq, D_MODEL), pool_pre, conv_pre, *weights)
    return (y.reshape(batch, seq, D_MODEL), pool_state[:, POOL_PAD - POOL_STATE:],
            conv_state[:, CONV_PAD - (CONV_W - 1):], v.reshape(batch, seq, D_A))


def _stacked_weights(ln_mix_g, ln_mix_b, w_in, v_norm_g, v_norm_b, w_spatial, b_spatial, w_pool, pool_scale,
                     conv_w, w_out, ln_ffn_g, ln_ffn_b, w_gate, w_up, w_down):
    depth = w_in.shape[0]
    row = lambda a: a.reshape(depth, 1, -1)
    ws = w_spatial.reshape(depth, N_PAIRS_A, 2, MLP_CHUNK, MLP_CHUNK).transpose(0, 1, 3, 2, 4)
    ws = ws.reshape(depth, N_PAIRS_A, MLP_CHUNK, 2 * MLP_CHUNK)
    bz = jnp.repeat(b_spatial.transpose(0, 2, 1), HEAD_DIM, axis=2)
    n_groups = len(POOL_WINDOWS)
    same_group = jnp.eye(n_groups, dtype=w_pool.dtype)[None, :, None, :, None]
    wpool = (w_pool[:, :, :, None, :] * same_group).reshape(depth, D_B, D_B).astype(_BF16)
    return (w_in.astype(_BF16), ws, bz, row(v_norm_g), row(v_norm_b), wpool, row(pool_scale),
            conv_w, w_out.astype(_BF16), row(ln_mix_g), row(ln_mix_b),
            w_gate.astype(_BF16), w_up.astype(_BF16), w_down.astype(_BF16),
            row(ln_ffn_g), row(ln_ffn_b))


def kernel(x_prompt, x_sample, state_pool, state_conv, ln_mix_g, ln_mix_b, w_in, v_norm_g, v_norm_b, w_spatial, b_spatial, w_pool, pool_scale, conv_w, w_out, ln_ffn_g, ln_ffn_b, w_gate, w_up, w_down):
    weights = _stacked_weights(ln_mix_g, ln_mix_b, w_in, v_norm_g, v_norm_b, w_spatial, b_spatial, w_pool,
                               pool_scale, conv_w, w_out, ln_ffn_g, ln_ffn_b, w_gate, w_up, w_down)
    yp, ys = x_prompt, x_sample
    pool_p, pool_s, conv_p, conv_s, v_s = [], [], [], [], []
    for l in range(DEPTH):
        yp, pp, cp = _prompt_layer(l, yp, weights)
        ys, ps, cs, vs = _sample_layer(l, ys, state_pool[l], state_conv[l], weights)
        pool_p.append(pp)
        conv_p.append(cp)
        pool_s.append(ps)
        conv_s.append(cs)
        v_s.append(vs)
    return (yp, ys, jnp.stack(pool_p), jnp.stack(pool_s), jnp.stack(conv_p), jnp.stack(conv_s), jnp.stack(v_s))
```

```python
import functools

import jax
import jax.numpy as jnp
from jax import lax
from jax.experimental import pallas as pl
from jax.experimental.pallas import tpu as pltpu

D_MODEL = 1024
DEPTH = 2
PAST_LEN = 4096
CHUNK = 64
MLP_CHUNK = 128
HEAD_DIM = 64
D_A = 512
N_HEADS_A = D_A // HEAD_DIM
N_PAIRS_A = N_HEADS_A // 2
D_B = 256
POOL_WINDOWS = (2, 4, 8, 16)
POOL_GROUP_DIM = D_B // len(POOL_WINDOWS)
POOL_STATE = max(POOL_WINDOWS) - 1
D_C = 256
CONV_W = 3
D_IN = 2 * D_A + D_B + 3 * D_C
D_FF = 2816
ALPHA = (2 * DEPTH) ** 0.25
LN_EPS = 1e-5

LANES = 128
POOL_PAD = 16
CONV_PAD = 8
FF_BLOCK = 256
ROW_BLOCK = 64
FFN_ROWS = 256
TILE_ROWS = 512
VMEM_LIMIT_BYTES = 56 * 1024 * 1024

_U0, _V0, _XB0, _GB0, _GC0, _H0 = 0, D_A, 2 * D_A, 2 * D_A + D_B, 2 * D_A + D_B + D_C, 2 * D_A + D_B + 2 * D_C
_OA0, _OB0, _OC0 = 0, D_A, D_A + D_B

_F32 = jnp.float32
_BF16 = jnp.bfloat16


def _layer_norm(x, g, b):
    mu = jnp.mean(x, axis=-1, keepdims=True)
    xc = x - mu
    var = jnp.mean(xc * xc, axis=-1, keepdims=True)
    return xc * lax.rsqrt(var + LN_EPS) * g + b


def _dot(a, b):
    return jnp.dot(a, b, preferred_element_type=_F32)


def _first_step_init(first, ws_ref, wsm_ref, *zero_refs):
    def mask_pair(p, carry):
        i_blk = lax.broadcasted_iota(jnp.int32, (MLP_CHUNK, 2 * MLP_CHUNK), 0) // CHUNK
        j_blk = (lax.broadcasted_iota(jnp.int32, (MLP_CHUNK, 2 * MLP_CHUNK), 1) % MLP_CHUNK) // CHUNK
        wsm_ref[p] = jnp.where(j_blk <= i_blk, ws_ref[p], 0.0).astype(_BF16)
        return carry

    lax.fori_loop(0, jnp.where(first, N_PAIRS_A, 0), mask_pair, 0)
    for ref in zero_refs:
        def zero_rows(i, carry, ref=ref):
            ref[pl.ds(pl.multiple_of(i * 8, 8), 8), :] = jnp.zeros((8, ref.shape[1]), ref.dtype)
            return carry

        lax.fori_loop(0, jnp.where(first, ref.shape[0] // 8, 0), zero_rows, 0)


def _layer_step(x_ref, w_in_ref, ws_ref, bz_ref, vg_ref, vb_ref, wpool_ref, pscale_ref, convw_ref,
                w_out_ref, g1_ref, b1_ref, wg_ref, wu_ref, wd_ref, g2_ref, b2_ref,
                y_ref, v_out_ref, proj_ref, cat_ref, m_ref, x1b_ref, hid_ref, ffn_ref, wsm_ref,
                *, pipelined, seg_len, pos0, pool_prefix, conv_prefix):
    tile_rows = x_ref.shape[0]
    n_seg = tile_rows // seg_len
    rows_a = min(MLP_CHUNK, seg_len)

    def in_proj():
        proj_ref[...] = _dot(x_ref[...].astype(_BF16), w_in_ref[...])

    def chunk_mlp(c):
        low_half = lax.broadcasted_iota(jnp.int32, (rows_a, LANES), 1) < HEAD_DIM
        zero_rows = jnp.zeros((MLP_CHUNK - rows_a, LANES), _BF16)
        rows = pl.ds(c * rows_a, rows_a)
        v = _layer_norm(proj_ref[rows, _V0:_V0 + D_A], vg_ref[...], vb_ref[...])
        if v_out_ref is not None:
            v_out_ref[rows, :] = v
        for p in range(N_PAIRS_A):
            lanes = slice(p * LANES, (p + 1) * LANES)
            vp = v[:, lanes]
            pieces = [jnp.where(low_half, vp, 0.0).astype(_BF16)]
            if rows_a < MLP_CHUNK:
                pieces.append(zero_rows)
            pieces.append(jnp.where(low_half, 0.0, vp).astype(_BF16))
            if rows_a < MLP_CHUNK:
                pieces.append(zero_rows)
            rhs = jnp.concatenate(pieces, axis=0)
            z = _dot(wsm_ref[p, :rows_a, :], rhs) + bz_ref[:rows_a, lanes]
            u = proj_ref[rows, _U0 + p * LANES:_U0 + (p + 1) * LANES]
            cat_ref[rows, _OA0 + p * LANES:_OA0 + (p + 1) * LANES] = (u * z).astype(_BF16)

    def pool_conv(s):
        low_half = lax.broadcasted_iota(jnp.int32, (seg_len, LANES), 1) < POOL_GROUP_DIM
        pos = pos0 + lax.broadcasted_iota(jnp.int32, (seg_len, LANES), 0)
        rows = pl.ds(s * seg_len, seg_len)
        pre = pool_prefix(s)
        for half in range(D_B // LANES):
            lanes = slice(half * LANES, (half + 1) * LANES)
            xb = proj_ref[rows, _XB0 + half * LANES:_XB0 + (half + 1) * LANES]
            padded = jnp.concatenate([pre[:, lanes], xb], axis=0)
            s2 = padded + pltpu.roll(padded, 1, 0)
            s4 = s2 + pltpu.roll(s2, 2, 0)
            if half == 0:
                win_a, win_b, w_a, w_b = s2, s4, POOL_WINDOWS[0], POOL_WINDOWS[1]
            else:
                s8 = s4 + pltpu.roll(s4, 4, 0)
                s16 = s8 + pltpu.roll(s8, 8, 0)
                win_a, win_b, w_a, w_b = s8, s16, POOL_WINDOWS[2], POOL_WINDOWS[3]
            win = jnp.where(low_half, win_a[POOL_PAD:], win_b[POOL_PAD:])
            cnt = jnp.where(low_half, jnp.minimum(pos + 1, w_a), jnp.minimum(pos + 1, w_b)).astype(_F32)
            cat_ref[rows, _OB0 + half * LANES:_OB0 + (half + 1) * LANES] = (win / cnt - xb).astype(_BF16)

        zc = proj_ref[rows, _GC0:_GC0 + D_C] * proj_ref[rows, _H0:_H0 + D_C]
        padded = jnp.concatenate([conv_prefix(s), zc], axis=0)
        y = (convw_ref[0:1, :] * pltpu.roll(padded, 2, 0)[CONV_PAD:]
             + convw_ref[1:2, :] * pltpu.roll(padded, 1, 0)[CONV_PAD:]
             + convw_ref[2:3, :] * zc)
        cat_ref[rows, _OC0:_OC0 + D_C] = (proj_ref[rows, _GB0:_GB0 + D_C] * y).astype(_BF16)

    def pool_map():
        mapped = _dot(cat_ref[:, _OB0:_OB0 + D_B], wpool_ref[...]) * pscale_ref[...]
        cat_ref[:, _OB0:_OB0 + D_B] = mapped.astype(_BF16)

    def out_proj():
        m_ref[...] = ALPHA * x_ref[...] + _dot(cat_ref[...], w_out_ref[...])

    def mix_norm(r):
        rows = pl.ds(r * ROW_BLOCK, ROW_BLOCK)
        x1 = _layer_norm(m_ref[rows, :], g1_ref[...], b1_ref[...])
        y_ref[rows, :] = x1
        x1b_ref[rows, :] = x1.astype(_BF16)

    def ffn_hidden(c, blk=0, n_rows=None):
        rows = pl.ds(blk * (n_rows or tile_rows), n_rows or tile_rows)
        cols = slice(c * FF_BLOCK, (c + 1) * FF_BLOCK)
        gate = _dot(x1b_ref[rows, :], wg_ref[:, cols])
        up = _dot(x1b_ref[rows, :], wu_ref[:, cols])
        hid_ref[rows, cols] = (gate * (1.0 / (1.0 + jnp.exp(-gate))) * up).astype(_BF16)

    def ffn_down(blk):
        rows = pl.ds(blk * FFN_ROWS, FFN_ROWS)
        ffn_ref[rows, :] = ALPHA * y_ref[rows, :] + _dot(hid_ref[rows, :], wd_ref[...])

    def ffn_norm(r):
        rows = pl.ds(r * ROW_BLOCK, ROW_BLOCK)
        y_ref[rows, :] = _layer_norm(ffn_ref[rows, :], g2_ref[...], b2_ref[...])

    mixers = ([functools.partial(chunk_mlp, c) for c in range(tile_rows // rows_a)]
              + [functools.partial(pool_conv, s) for s in range(n_seg)] + [pool_map])
    mix_norms = [functools.partial(mix_norm, r) for r in range(tile_rows // ROW_BLOCK)]
    ffn_norms = [functools.partial(ffn_norm, r) for r in range(tile_rows // ROW_BLOCK)]
    n_ff = D_FF // FF_BLOCK

    if not pipelined:
        order = ([in_proj] + mixers + [out_proj] + mix_norms
                 + [functools.partial(ffn_hidden, c) for c in range(n_ff)]
                 + [functools.partial(ffn_down, blk) for blk in range(tile_rows // FFN_ROWS)] + ffn_norms)
        for piece in order:
            piece()
        return

    n_down = tile_rows // FFN_ROWS
    per_block = len(ffn_norms) // n_down
    in_proj()
    done = 0
    for piece in mix_norms[:per_block]:
        piece()
    later_norms = list(range(per_block, len(mix_norms)))
    for blk in range(n_down):
        for c in range(n_ff):
            ffn_hidden(c, blk, FFN_ROWS)
            if blk == 0 and c >= 1 and later_norms:
                r = later_norms.pop(0)
                bits = pltpu.bitcast(hid_ref[0:16, c * FF_BLOCK:c * FF_BLOCK + LANES], jnp.uint32)
                zero = lax.shift_right_logical(lax.shift_right_logical(bits, jnp.uint32(16)), jnp.uint32(16))
                first = (pl.ds(r * ROW_BLOCK, 8), pl.ds(0, LANES))
                m_ref[first] = m_ref[first] + pltpu.bitcast(zero, _F32)
                mix_norms[r]()
            upto = (len(mixers) * (blk * n_ff + c + 1)) // (n_down * n_ff)
            for piece in mixers[done:upto]:
                piece()
            done = upto
    for blk in range(n_down):
        ffn_down(blk)
        if blk == n_down - 1:
            out_proj()
        for piece in ffn_norms[blk * per_block:(blk + 1) * per_block]:
            piece()


def _state_tails(proj_ref, s, seg_len):
    pool_tail = proj_ref[pl.ds((s + 1) * seg_len - POOL_PAD, POOL_PAD), _XB0:_XB0 + D_B]
    rows = pl.ds((s + 1) * seg_len - CONV_PAD, CONV_PAD)
    return pool_tail, proj_ref[rows, _GC0:_GC0 + D_C] * proj_ref[rows, _H0:_H0 + D_C]


_N_WEIGHTS = 16


def _prompt_kernel(x_ref, *refs, n_tiles, n_steps):
    weights = refs[:_N_WEIGHTS]
    y_ref, pool_out_ref, conv_out_ref = refs[_N_WEIGHTS:_N_WEIGHTS + 3]
    (proj_ref, cat_ref, m_ref, x1b_ref, hid_ref, ffn_ref, wsm_ref,
     pool_carry_ref, conv_carry_ref) = refs[_N_WEIGHTS + 3:]
    step = pl.program_id(0)
    t = jnp.minimum(step, n_steps - 1) % n_tiles
    tile_rows = x_ref.shape[0]

    _first_step_init(step == 0, weights[1], wsm_ref, m_ref)

    @pl.when(t == 0)
    def _():
        pool_carry_ref[...] = jnp.zeros_like(pool_carry_ref)
        conv_carry_ref[...] = jnp.zeros_like(conv_carry_ref)

    _layer_step(x_ref, *weights, y_ref, None, proj_ref, cat_ref, m_ref, x1b_ref, hid_ref, ffn_ref, wsm_ref,
                pipelined=True, seg_len=tile_rows, pos0=t * tile_rows,
                pool_prefix=lambda s: pool_carry_ref[...], conv_prefix=lambda s: conv_carry_ref[...])

    @pl.when(step < n_steps)
    def _():
        pool_tail, conv_tail = _state_tails(proj_ref, 0, tile_rows)
        pool_carry_ref[...] = pool_tail
        conv_carry_ref[...] = conv_tail
        pool_out_ref[0] = pool_tail
        conv_out_ref[0] = conv_tail


def _sample_kernel(x_ref, pool_pre_ref, conv_pre_ref, *refs, seg_len):
    weights = refs[:_N_WEIGHTS]
    y_ref, pool_out_ref, conv_out_ref, v_out_ref = refs[_N_WEIGHTS:_N_WEIGHTS + 4]
    proj_ref, cat_ref, m_ref, x1b_ref, hid_ref, ffn_ref, wsm_ref = refs[_N_WEIGHTS + 4:]

    _first_step_init(pl.program_id(0) == 0, weights[1], wsm_ref)

    _layer_step(x_ref, *weights, y_ref, v_out_ref, proj_ref, cat_ref, m_ref, x1b_ref, hid_ref, ffn_ref, wsm_ref,
                pipelined=False, seg_len=seg_len, pos0=PAST_LEN,
                pool_prefix=lambda s: pool_pre_ref[s], conv_prefix=lambda s: conv_pre_ref[s])

    for s in range(x_ref.shape[0] // seg_len):
        pool_out_ref[s], conv_out_ref[s] = _state_tails(proj_ref, s, seg_len)


def _resident(layer, shape):
    return pl.BlockSpec((None,) + tuple(shape[1:]), lambda *_: (layer,) + (0,) * (len(shape) - 1),
                        pipeline_mode=pl.Buffered(1))


def _tile_scratch(tile_rows):
    return [
        pltpu.VMEM((tile_rows, D_IN), _F32),
        pltpu.VMEM((tile_rows, D_MODEL), _BF16),
        pltpu.VMEM((tile_rows, D_MODEL), _F32),
        pltpu.VMEM((tile_rows, D_MODEL), _BF16),
        pltpu.VMEM((tile_rows, D_FF), _BF16),
        pltpu.VMEM((tile_rows, D_MODEL), _F32),
        pltpu.VMEM((N_PAIRS_A, MLP_CHUNK, 2 * MLP_CHUNK), _BF16),
    ]


def _prompt_layer(layer, x, weights):
    batch, seq, _ = x.shape
    tile_rows = min(TILE_ROWS, seq)
    n_tiles = seq // tile_rows
    n_steps = batch * n_tiles
    assert seq % tile_rows == 0 and tile_rows % MLP_CHUNK == 0
    mix_tile = lambda i: jnp.minimum(i, n_steps - 1)
    ffn_tile = lambda i: jnp.maximum(i - 1, 0)
    y, pool_state, conv_state = pl.pallas_call(
        functools.partial(_prompt_kernel, n_tiles=n_tiles, n_steps=n_steps),
        grid=(n_steps + 1,),
        in_specs=[pl.BlockSpec((tile_rows, D_MODEL), lambda i: (mix_tile(i), 0))]
                 + [_resident(layer, w.shape) for w in weights],
        out_specs=[pl.BlockSpec((tile_rows, D_MODEL), lambda i: (ffn_tile(i), 0)),
                   pl.BlockSpec((1, POOL_PAD, D_B), lambda i: (mix_tile(i) // n_tiles, 0, 0)),
                   pl.BlockSpec((1, CONV_PAD, D_C), lambda i: (mix_tile(i) // n_tiles, 0, 0))],
        out_shape=[jax.ShapeDtypeStruct((batch * seq, D_MODEL), _F32),
                   jax.ShapeDtypeStruct((batch, POOL_PAD, D_B), _F32),
                   jax.ShapeDtypeStruct((batch, CONV_PAD, D_C), _F32)],
        scratch_shapes=_tile_scratch(tile_rows) + [pltpu.VMEM((POOL_PAD, D_B), _F32),
                                                   pltpu.VMEM((CONV_PAD, D_C), _F32)],
        compiler_params=pltpu.CompilerParams(dimension_semantics=("arbitrary",),
                                             vmem_limit_bytes=VMEM_LIMIT_BYTES),
        name="prompt_layer",
    )(x.reshape(batch * seq, D_MODEL), *weights)
    return y.reshape(batch, seq, D_MODEL), pool_state[:, POOL_PAD - POOL_STATE:], conv_state[:, CONV_PAD - (CONV_W - 1):]


def _sample_layer(layer, x, pool_prefix, conv_prefix, weights):
    batch, seq, _ = x.shape
    assert seq % POOL_PAD == 0 and (seq % MLP_CHUNK == 0 or MLP_CHUNK % seq == 0 and seq % CHUNK == 0)
    seg_per_tile = max(1, min(batch, TILE_ROWS // seq))
    assert batch % seg_per_tile == 0
    tile_rows = seg_per_tile * seq
    pool_pre = jnp.pad(pool_prefix, ((0, 0), (POOL_PAD - POOL_STATE, 0), (0, 0)))
    conv_pre = jnp.pad(conv_prefix, ((0, 0), (CONV_PAD - (CONV_W - 1), 0), (0, 0)))
    y, pool_state, conv_state, v = pl.pallas_call(
        functools.partial(_sample_kernel, seg_len=seq),
        grid=(batch // seg_per_tile,),
        in_specs=[pl.BlockSpec((tile_rows, D_MODEL), lambda i: (i, 0)),
                  pl.BlockSpec((seg_per_tile, POOL_PAD, D_B), lambda i: (i, 0, 0)),
                  pl.BlockSpec((seg_per_tile, CONV_PAD, D_C), lambda i: (i, 0, 0))]
                 + [_resident(layer, w.shape) for w in weights],
        out_specs=[pl.BlockSpec((tile_rows, D_MODEL), lambda i: (i, 0)),
                   pl.BlockSpec((seg_per_tile, POOL_PAD, D_B), lambda i: (i, 0, 0)),
                   pl.BlockSpec((seg_per_tile, CONV_PAD, D_C), lambda i: (i, 0, 0)),
                   pl.BlockSpec((tile_rows, D_A), lambda i: (i, 0))],
        out_shape=[jax.ShapeDtypeStruct((batch * seq, D_MODEL), _F32),
                   jax.ShapeDtypeStruct((batch, POOL_PAD, D_B), _F32),
                   jax.ShapeDtypeStruct((batch, CONV_PAD, D_C), _F32),
                   jax.ShapeDtypeStruct((batch * seq, D_A), _F32)],
        scratch_shapes=_tile_scratch(tile_rows),
        compiler_params=pltpu.CompilerParams(dimension_semantics=("arbitrary",),
                                             vmem_limit_bytes=VMEM_LIMIT_BYTES),
        name="sample_layer",
    )(x.reshape(batch * seq, D_MODEL), pool_pre, conv_pre, *weights)
    return (y.reshape(batch, seq, D_MODEL), pool_state[:, POOL_PAD - POOL_STATE:],
            conv_state[:, CONV_PAD - (CONV_W - 1):], v.reshape(batch, seq, D_A))


def _stacked_weights(ln_mix_g, ln_mix_b, w_in, v_norm_g, v_norm_b, w_spatial, b_spatial, w_pool, pool_scale,
                     conv_w, w_out, ln_ffn_g, ln_ffn_b, w_gate, w_up, w_down):
    depth = w_in.shape[0]
    row = lambda a: a.reshape(depth, 1, -1)
    ws = w_spatial.reshape(depth, N_PAIRS_A, 2, MLP_CHUNK, MLP_CHUNK).transpose(0, 1, 3, 2, 4)
    ws = ws.reshape(depth, N_PAIRS_A, MLP_CHUNK, 2 * MLP_CHUNK)
    bz = jnp.repeat(b_spatial.transpose(0, 2, 1), HEAD_DIM, axis=2)
    n_groups = len(POOL_WINDOWS)
    same_group = jnp.eye(n_groups, dtype=w_pool.dtype)[None, :, None, :, None]
    wpool = (w_pool[:, :, :, None, :] * same_group).reshape(depth, D_B, D_B).astype(_BF16)
    return (w_in.astype(_BF16), ws, bz, row(v_norm_g), row(v_norm_b), wpool, row(pool_scale),
            conv_w, w_out.astype(_BF16), row(ln_mix_g), row(ln_mix_b),
            w_gate.astype(_BF16), w_up.astype(_BF16), w_down.astype(_BF16),
            row(ln_ffn_g), row(ln_ffn_b))


def kernel(x_prompt, x_sample, state_pool, state_conv, ln_mix_g, ln_mix_b, w_in, v_norm_g, v_norm_b, w_spatial, b_spatial, w_pool, pool_scale, conv_w, w_out, ln_ffn_g, ln_ffn_b, w_gate, w_up, w_down):
    weights = _stacked_weights(ln_mix_g, ln_mix_b, w_in, v_norm_g, v_norm_b, w_spatial, b_spatial, w_pool,
                               pool_scale, conv_w, w_out, ln_ffn_g, ln_ffn_b, w_gate, w_up, w_down)
    yp, ys = x_prompt, x_sample
    pool_p, pool_s, conv_p, conv_s, v_s = [], [], [], [], []
    for l in range(DEPTH):
        yp, pp, cp = _prompt_layer(l, yp, weights)
        ys, ps, cs, vs = _sample_layer(l, ys, state_pool[l], state_conv[l], weights)
        pool_p.append(pp)
        conv_p.append(cp)
        pool_s.append(ps)
        conv_s.append(cs)
        v_s.append(vs)
    return (yp, ys, jnp.stack(pool_p), jnp.stack(pool_s), jnp.stack(conv_p), jnp.stack(conv_s), jnp.stack(v_s))
```

```python
import functools

import jax
import jax.numpy as jnp
from jax import lax
from jax.experimental import pallas as pl
from jax.experimental.pallas import tpu as pltpu

D_MODEL = 1024
DEPTH = 2
PAST_LEN = 4096
CHUNK = 64
MLP_CHUNK = 128
HEAD_DIM = 64
D_A = 512
N_HEADS_A = D_A // HEAD_DIM
N_PAIRS_A = N_HEADS_A // 2
D_B = 256
POOL_WINDOWS = (2, 4, 8, 16)
POOL_GROUP_DIM = D_B // len(POOL_WINDOWS)
POOL_STATE = max(POOL_WINDOWS) - 1
D_C = 256
CONV_W = 3
D_IN = 2 * D_A + D_B + 3 * D_C
D_FF = 2816
ALPHA = (2 * DEPTH) ** 0.25
LN_EPS = 1e-5

LANES = 128
POOL_PAD = 16
CONV_PAD = 8
FF_BLOCK = 256
ROW_BLOCK = 64
FFN_ROWS = 256
TILE_ROWS = 512
VMEM_LIMIT_BYTES = 56 * 1024 * 1024

_U0, _V0, _XB0, _GB0, _GC0, _H0 = 0, D_A, 2 * D_A, 2 * D_A + D_B, 2 * D_A + D_B + D_C, 2 * D_A + D_B + 2 * D_C
_OA0, _OB0, _OC0 = 0, D_A, D_A + D_B

_F32 = jnp.float32
_BF16 = jnp.bfloat16


def _layer_norm(x, g, b):
    mu = jnp.mean(x, axis=-1, keepdims=True)
    xc = x - mu
    var = jnp.mean(xc * xc, axis=-1, keepdims=True)
    return xc * lax.rsqrt(var + LN_EPS) * g + b


def _dot(a, b):
    return jnp.dot(a, b, preferred_element_type=_F32)


def _first_step_init(first, ws_ref, wsm_ref, *zero_refs):
    def mask_pair(p, carry):
        i_blk = lax.broadcasted_iota(jnp.int32, (MLP_CHUNK, 2 * MLP_CHUNK), 0) // CHUNK
        j_blk = (lax.broadcasted_iota(jnp.int32, (MLP_CHUNK, 2 * MLP_CHUNK), 1) % MLP_CHUNK) // CHUNK
        wsm_ref[p] = jnp.where(j_blk <= i_blk, ws_ref[p], 0.0).astype(_BF16)
        return carry

    lax.fori_loop(0, jnp.where(first, N_PAIRS_A, 0), mask_pair, 0)
    for ref in zero_refs:
        def zero_rows(i, carry, ref=ref):
            ref[pl.ds(pl.multiple_of(i * 8, 8), 8), :] = jnp.zeros((8, ref.shape[1]), ref.dtype)
            return carry

        lax.fori_loop(0, jnp.where(first, ref.shape[0] // 8, 0), zero_rows, 0)


def _layer_step(x_ref, w_in_ref, ws_ref, bz_ref, vg_ref, vb_ref, wpool_ref, pscale_ref, convw_ref,
                w_out_ref, g1_ref, b1_ref, wg_ref, wu_ref, wd_ref, g2_ref, b2_ref,
                y_ref, v_out_ref, proj_ref, cat_ref, m_ref, x1b_ref, hid_ref, ffn_ref, wsm_ref,
                *, pipelined, seg_len, pos0, pool_prefix, conv_prefix):
    tile_rows = x_ref.shape[0]
    n_seg = tile_rows // seg_len
    rows_a = min(MLP_CHUNK, seg_len)

    def in_proj():
        proj_ref[...] = _dot(x_ref[...].astype(_BF16), w_in_ref[...])

    def chunk_mlp(c):
        low_half = lax.broadcasted_iota(jnp.int32, (rows_a, LANES), 1) < HEAD_DIM
        zero_rows = jnp.zeros((MLP_CHUNK - rows_a, LANES), _BF16)
        rows = pl.ds(c * rows_a, rows_a)
        v = _layer_norm(proj_ref[rows, _V0:_V0 + D_A], vg_ref[...], vb_ref[...])
        if v_out_ref is not None:
            v_out_ref[rows, :] = v
        for p in range(N_PAIRS_A):
            lanes = slice(p * LANES, (p + 1) * LANES)
            vp = v[:, lanes]
            pieces = [jnp.where(low_half, vp, 0.0).astype(_BF16)]
            if rows_a < MLP_CHUNK:
                pieces.append(zero_rows)
            pieces.append(jnp.where(low_half, 0.0, vp).astype(_BF16))
            if rows_a < MLP_CHUNK:
                pieces.append(zero_rows)
            rhs = jnp.concatenate(pieces, axis=0)
            z = _dot(wsm_ref[p, :rows_a, :], rhs) + bz_ref[:rows_a, lanes]
            u = proj_ref[rows, _U0 + p * LANES:_U0 + (p + 1) * LANES]
            cat_ref[rows, _OA0 + p * LANES:_OA0 + (p + 1) * LANES] = (u * z).astype(_BF16)

    def pool_conv(s):
        low_half = lax.broadcasted_iota(jnp.int32, (seg_len, LANES), 1) < POOL_GROUP_DIM
        pos = pos0 + lax.broadcasted_iota(jnp.int32, (seg_len, LANES), 0)
        rows = pl.ds(s * seg_len, seg_len)
        pre = pool_prefix(s)
        for half in range(D_B // LANES):
            lanes = slice(half * LANES, (half + 1) * LANES)
            xb = proj_ref[rows, _XB0 + half * LANES:_XB0 + (half + 1) * LANES]
            padded = jnp.concatenate([pre[:, lanes], xb], axis=0)
            s2 = padded + pltpu.roll(padded, 1, 0)
            s4 = s2 + pltpu.roll(s2, 2, 0)
            if half == 0:
                win_a, win_b, w_a, w_b = s2, s4, POOL_WINDOWS[0], POOL_WINDOWS[1]
            else:
                s8 = s4 + pltpu.roll(s4, 4, 0)
                s16 = s8 + pltpu.roll(s8, 8, 0)
                win_a, win_b, w_a, w_b = s8, s16, POOL_WINDOWS[2], POOL_WINDOWS[3]
            win = jnp.where(low_half, win_a[POOL_PAD:], win_b[POOL_PAD:])
            cnt = jnp.where(low_half, jnp.minimum(pos + 1, w_a), jnp.minimum(pos + 1, w_b)).astype(_F32)
            cat_ref[rows, _OB0 + half * LANES:_OB0 + (half + 1) * LANES] = (win / cnt - xb).astype(_BF16)

        zc = proj_ref[rows, _GC0:_GC0 + D_C] * proj_ref[rows, _H0:_H0 + D_C]
        padded = jnp.concatenate([conv_prefix(s), zc], axis=0)
        y = (convw_ref[0:1, :] * pltpu.roll(padded, 2, 0)[CONV_PAD:]
             + convw_ref[1:2, :] * pltpu.roll(padded, 1, 0)[CONV_PAD:]
             + convw_ref[2:3, :] * zc)
        cat_ref[rows, _OC0:_OC0 + D_C] = (proj_ref[rows, _GB0:_GB0 + D_C] * y).astype(_BF16)

    def pool_map():
        mapped = _dot(cat_ref[:, _OB0:_OB0 + D_B], wpool_ref[...]) * pscale_ref[...]
        cat_ref[:, _OB0:_OB0 + D_B] = mapped.astype(_BF16)

    def out_proj():
        m_ref[...] = ALPHA * x_ref[...] + _dot(cat_ref[...], w_out_ref[...])

    def mix_norm(r):
        rows = pl.ds(r * ROW_BLOCK, ROW_BLOCK)
        x1 = _layer_norm(m_ref[rows, :], g1_ref[...], b1_ref[...])
        y_ref[rows, :] = x1
        x1b_ref[rows, :] = x1.astype(_BF16)

    def ffn_hidden(c, blk=0, n_rows=None):
        rows = pl.ds(blk * (n_rows or tile_rows), n_rows or tile_rows)
        cols = slice(c * FF_BLOCK, (c + 1) * FF_BLOCK)
        gate = _dot(x1b_ref[rows, :], wg_ref[:, cols])
        up = _dot(x1b_ref[rows, :], wu_ref[:, cols])
        hid_ref[rows, cols] = (gate * (1.0 / (1.0 + jnp.exp(-gate))) * up).astype(_BF16)

    def ffn_down(blk):
        rows = pl.ds(blk * FFN_ROWS, FFN_ROWS)
        ffn_ref[rows, :] = ALPHA * y_ref[rows, :] + _dot(hid_ref[rows, :], wd_ref[...])

    def ffn_norm(r):
        rows = pl.ds(r * ROW_BLOCK, ROW_BLOCK)
        y_ref[rows, :] = _layer_norm(ffn_ref[rows, :], g2_ref[...], b2_ref[...])

    mixers = ([functools.partial(chunk_mlp, c) for c in range(tile_rows // rows_a)]
              + [functools.partial(pool_conv, s) for s in range(n_seg)] + [pool_map])
    mix_norms = [functools.partial(mix_norm, r) for r in range(tile_rows // ROW_BLOCK)]
    ffn_norms = [functools.partial(ffn_norm, r) for r in range(tile_rows // ROW_BLOCK)]
    n_ff = D_FF // FF_BLOCK

    if not pipelined:
        order = ([in_proj] + mixers + [out_proj] + mix_norms
                 + [functools.partial(ffn_hidden, c) for c in range(n_ff)]
                 + [functools.partial(ffn_down, blk) for blk in range(tile_rows // FFN_ROWS)] + ffn_norms)
        for piece in order:
            piece()
        return

    n_down = tile_rows // FFN_ROWS
    per_block = len(ffn_norms) // n_down
    in_proj()
    done = 0
    for piece in mix_norms[:per_block]:
        piece()
    later_norms = list(range(per_block, len(mix_norms)))
    for blk in range(n_down):
        for c in range(n_ff):
            ffn_hidden(c, blk, FFN_ROWS)
            if blk == 0 and c >= 1 and later_norms:
                r = later_norms.pop(0)
                bits = pltpu.bitcast(hid_ref[0:16, c * FF_BLOCK:c * FF_BLOCK + LANES], jnp.uint32)
                zero = lax.shift_right_logical(lax.shift_right_logical(bits, jnp.uint32(16)), jnp.uint32(16))
                first = (pl.ds(r * ROW_BLOCK, 8), pl.ds(0, LANES))
                m_ref[first] = m_ref[first] + pltpu.bitcast(zero, _F32)
                mix_norms[r]()
            upto = (len(mixers) * (blk * n_ff + c + 1)) // (n_down * n_ff)
            for piece in mixers[done:upto]:
                piece()
            done = upto
    assert not later_norms and done == len(mixers)
    for blk in range(n_down):
        ffn_down(blk)
        if blk == n_down - 1:
            out_proj()
        for piece in ffn_norms[blk * per_block:(blk + 1) * per_block]:
            piece()


def _state_tails(proj_ref, s, seg_len):
    pool_tail = proj_ref[pl.ds((s + 1) * seg_len - POOL_PAD, POOL_PAD), _XB0:_XB0 + D_B]
    rows = pl.ds((s + 1) * seg_len - CONV_PAD, CONV_PAD)
    return pool_tail, proj_ref[rows, _GC0:_GC0 + D_C] * proj_ref[rows, _H0:_H0 + D_C]


_N_WEIGHTS = 16


def _prompt_kernel(x_ref, *refs, n_tiles, n_steps):
    weights = refs[:_N_WEIGHTS]
    y_ref, pool_out_ref, conv_out_ref = refs[_N_WEIGHTS:_N_WEIGHTS + 3]
    (proj_ref, cat_ref, m_ref, x1b_ref, hid_ref, ffn_ref, wsm_ref,
     pool_carry_ref, conv_carry_ref) = refs[_N_WEIGHTS + 3:]
    step = pl.program_id(0)
    t = jnp.minimum(step, n_steps - 1) % n_tiles
    tile_rows = x_ref.shape[0]

    _first_step_init(step == 0, weights[1], wsm_ref, m_ref)

    @pl.when(t == 0)
    def _():
        pool_carry_ref[...] = jnp.zeros_like(pool_carry_ref)
        conv_carry_ref[...] = jnp.zeros_like(conv_carry_ref)

    _layer_step(x_ref, *weights, y_ref, None, proj_ref, cat_ref, m_ref, x1b_ref, hid_ref, ffn_ref, wsm_ref,
                pipelined=True, seg_len=tile_rows, pos0=t * tile_rows,
                pool_prefix=lambda s: pool_carry_ref[...], conv_prefix=lambda s: conv_carry_ref[...])

    @pl.when(step < n_steps)
    def _():
        pool_tail, conv_tail = _state_tails(proj_ref, 0, tile_rows)
        pool_carry_ref[...] = pool_tail
        conv_carry_ref[...] = conv_tail
        pool_out_ref[0] = pool_tail
        conv_out_ref[0] = conv_tail


def _sample_kernel(x_ref, pool_pre_ref, conv_pre_ref, *refs, seg_len):
    weights = refs[:_N_WEIGHTS]
    y_ref, pool_out_ref, conv_out_ref, v_out_ref = refs[_N_WEIGHTS:_N_WEIGHTS + 4]
    proj_ref, cat_ref, m_ref, x1b_ref, hid_ref, ffn_ref, wsm_ref = refs[_N_WEIGHTS + 4:]

    _first_step_init(pl.program_id(0) == 0, weights[1], wsm_ref)

    _layer_step(x_ref, *weights, y_ref, v_out_ref, proj_ref, cat_ref, m_ref, x1b_ref, hid_ref, ffn_ref, wsm_ref,
                pipelined=False, seg_len=seg_len, pos0=PAST_LEN,
                pool_prefix=lambda s: pool_pre_ref[s], conv_prefix=lambda s: conv_pre_ref[s])

    for s in range(x_ref.shape[0] // seg_len):
        pool_out_ref[s], conv_out_ref[s] = _state_tails(proj_ref, s, seg_len)


def _resident(layer, shape):
    return pl.BlockSpec((None,) + tuple(shape[1:]), lambda *_: (layer,) + (0,) * (len(shape) - 1),
                        pipeline_mode=pl.Buffered(1))


def _tile_scratch(tile_rows):
    return [
        pltpu.VMEM((tile_rows, D_IN), _F32),
        pltpu.VMEM((tile_rows, D_MODEL), _BF16),
        pltpu.VMEM((tile_rows, D_MODEL), _F32),
        pltpu.VMEM((tile_rows, D_MODEL), _BF16),
        pltpu.VMEM((tile_rows, D_FF), _BF16),
        pltpu.VMEM((tile_rows, D_MODEL), _F32),
        pltpu.VMEM((N_PAIRS_A, MLP_CHUNK, 2 * MLP_CHUNK), _BF16),
    ]


def _prompt_layer(layer, x, weights):
    batch, seq, _ = x.shape
    tile_rows = min(TILE_ROWS, seq)
    n_tiles = seq // tile_rows
    n_steps = batch * n_tiles
    assert seq % tile_rows == 0 and tile_rows % MLP_CHUNK == 0
    assert tile_rows % FFN_ROWS == 0 and FFN_ROWS % ROW_BLOCK == 0
    mix_tile = lambda i: jnp.minimum(i, n_steps - 1)
    ffn_tile = lambda i: jnp.maximum(i - 1, 0)
    y, pool_state, conv_state = pl.pallas_call(
        functools.partial(_prompt_kernel, n_tiles=n_tiles, n_steps=n_steps),
        grid=(n_steps + 1,),
        in_specs=[pl.BlockSpec((tile_rows, D_MODEL), lambda i: (mix_tile(i), 0))]
                 + [_resident(layer, w.shape) for w in weights],
        out_specs=[pl.BlockSpec((tile_rows, D_MODEL), lambda i: (ffn_tile(i), 0)),
                   pl.BlockSpec((1, POOL_PAD, D_B), lambda i: (mix_tile(i) // n_tiles, 0, 0)),
                   pl.BlockSpec((1, CONV_PAD, D_C), lambda i: (mix_tile(i) // n_tiles, 0, 0))],
        out_shape=[jax.ShapeDtypeStruct((batch * seq, D_MODEL), _F32),
                   jax.ShapeDtypeStruct((batch, POOL_PAD, D_B), _F32),
                   jax.ShapeDtypeStruct((batch, CONV_PAD, D_C), _F32)],
        scratch_shapes=_tile_scratch(tile_rows) + [pltpu.VMEM((POOL_PAD, D_B), _F32),
                                                   pltpu.VMEM((CONV_PAD, D_C), _F32)],
        compiler_params=pltpu.CompilerParams(dimension_semantics=("arbitrary",),
                                             vmem_limit_bytes=VMEM_LIMIT_BYTES),
        name="prompt_layer",
    )(x.reshape(batch * seq, D_MODEL), *weights)
    return y.reshape(batch, seq, D_MODEL), pool_state[:, POOL_PAD - POOL_STATE:], conv_state[:, CONV_PAD - (CONV_W - 1):]


def _sample_layer(layer, x, pool_prefix, conv_prefix, weights):
    batch, seq, _ = x.shape
    assert seq % POOL_PAD == 0 and (seq % MLP_CHUNK == 0 or MLP_CHUNK % seq == 0 and seq % CHUNK == 0)
    seg_per_tile = max(1, min(batch, TILE_ROWS // seq))
    assert batch % seg_per_tile == 0
    tile_rows = seg_per_tile * seq
    assert tile_rows % FFN_ROWS == 0 and FFN_ROWS % ROW_BLOCK == 0
    pool_pre = jnp.pad(pool_prefix, ((0, 0), (POOL_PAD - POOL_STATE, 0), (0, 0)))
    conv_pre = jnp.pad(conv_prefix, ((0, 0), (CONV_PAD - (CONV_W - 1), 0), (0, 0)))
    y, pool_state, conv_state, v = pl.pallas_call(
        functools.partial(_sample_kernel, seg_len=seq),
        grid=(batch // seg_per_tile,),
        in_specs=[pl.BlockSpec((tile_rows, D_MODEL), lambda i: (i, 0)),
                  pl.BlockSpec((seg_per_tile, POOL_PAD, D_B), lambda i: (i, 0, 0)),
                  pl.BlockSpec((seg_per_tile, CONV_PAD, D_C), lambda i: (i, 0, 0))]
                 + [_resident(layer, w.shape) for w in weights],
        out_specs=[pl.BlockSpec((tile_rows, D_MODEL), lambda i: (i, 0)),
                   pl.BlockSpec((seg_per_tile, POOL_PAD, D_B), lambda i: (i, 0, 0)),
                   pl.BlockSpec((seg_per_tile, CONV_PAD, D_C), lambda i: (i, 0, 0)),
                   pl.BlockSpec((tile_rows, D_A), lambda i: (i, 0))],
        out_shape=[jax.ShapeDtypeStruct((batch * seq, D_MODEL), _F32),
                   jax.ShapeDtypeStruct((batch, POOL_PAD, D_B), _F32),
                   jax.ShapeDtypeStruct((batch, CONV_PAD, D_C), _F32),
                   jax.ShapeDtypeStruct((batch * seq, D_A), _F32)],
        scratch_shapes=_tile_scratch(tile_rows),
        compiler_params=pltpu.CompilerParams(dimension_semantics=("arbitrary",),
                                             vmem_limit_bytes=VMEM_LIMIT_BYTES),
        name="sample_layer",
    )(x.reshape(batch * seq, D_MODEL), pool_pre, conv_pre, *weights)
    return (y.reshape(batch, seq, D_MODEL), pool_state[:, POOL_PAD - POOL_STATE:],
            conv_state[:, CONV_PAD - (CONV_W - 1):], v.reshape(batch, seq, D_A))


def _stacked_weights(ln_mix_g, ln_mix_b, w_in, v_norm_g, v_norm_b, w_spatial, b_spatial, w_pool, pool_scale,
                     conv_w, w_out, ln_ffn_g, ln_ffn_b, w_gate, w_up, w_down):
    depth = w_in.shape[0]
    row = lambda a: a.reshape(depth, 1, -1)
    ws = w_spatial.reshape(depth, N_PAIRS_A, 2, MLP_CHUNK, MLP_CHUNK).transpose(0, 1, 3, 2, 4)
    ws = ws.reshape(depth, N_PAIRS_A, MLP_CHUNK, 2 * MLP_CHUNK)
    bz = jnp.repeat(b_spatial.transpose(0, 2, 1), HEAD_DIM, axis=2)
    n_groups = len(POOL_WINDOWS)
    same_group = jnp.eye(n_groups, dtype=w_pool.dtype)[None, :, None, :, None]
    wpool = (w_pool[:, :, :, None, :] * same_group).reshape(depth, D_B, D_B).astype(_BF16)
    return (w_in.astype(_BF16), ws, bz, row(v_norm_g), row(v_norm_b), wpool, row(pool_scale),
            conv_w, w_out.astype(_BF16), row(ln_mix_g), row(ln_mix_b),
            w_gate.astype(_BF16), w_up.astype(_BF16), w_down.astype(_BF16),
            row(ln_ffn_g), row(ln_ffn_b))


def kernel(x_prompt, x_sample, state_pool, state_conv, ln_mix_g, ln_mix_b, w_in, v_norm_g, v_norm_b, w_spatial, b_spatial, w_pool, pool_scale, conv_w, w_out, ln_ffn_g, ln_ffn_b, w_gate, w_up, w_down):
    weights = _stacked_weights(ln_mix_g, ln_mix_b, w_in, v_norm_g, v_norm_b, w_spatial, b_spatial, w_pool,
                               pool_scale, conv_w, w_out, ln_ffn_g, ln_ffn_b, w_gate, w_up, w_down)
    yp, ys = x_prompt, x_sample
    pool_p, pool_s, conv_p, conv_s, v_s = [], [], [], [], []
    for l in range(DEPTH):
        yp, pp, cp = _prompt_layer(l, yp, weights)
        ys, ps, cs, vs = _sample_layer(l, ys, state_pool[l], state_conv[l], weights)
        pool_p.append(pp)
        conv_p.append(cp)
        pool_s.append(ps)
        conv_s.append(cs)
        v_s.append(vs)
    return (yp, ys, jnp.stack(pool_p), jnp.stack(pool_s), jnp.stack(conv_p), jnp.stack(conv_s), jnp.stack(v_s))
```

```python
import functools

import jax
import jax.numpy as jnp
from jax import lax
from jax.experimental import pallas as pl
from jax.experimental.pallas import tpu as pltpu

D_MODEL = 1024
DEPTH = 2
PAST_LEN = 4096
CHUNK = 64
MLP_CHUNK = 128
HEAD_DIM = 64
D_A = 512
N_HEADS_A = D_A // HEAD_DIM
N_PAIRS_A = N_HEADS_A // 2
D_B = 256
POOL_WINDOWS = (2, 4, 8, 16)
POOL_GROUP_DIM = D_B // len(POOL_WINDOWS)
POOL_STATE = max(POOL_WINDOWS) - 1
D_C = 256
CONV_W = 3
D_IN = 2 * D_A + D_B + 3 * D_C
D_FF = 2816
ALPHA = (2 * DEPTH) ** 0.25
LN_EPS = 1e-5

LANES = 128
POOL_PAD = 16
CONV_PAD = 8
FF_BLOCK = 256
ROW_BLOCK = 64
FFN_ROWS = 256
TILE_ROWS = 512
VMEM_LIMIT_BYTES = 56 * 1024 * 1024

_U0, _V0, _XB0, _GB0, _GC0, _H0 = 0, D_A, 2 * D_A, 2 * D_A + D_B, 2 * D_A + D_B + D_C, 2 * D_A + D_B + 2 * D_C
_OA0, _OB0, _OC0 = 0, D_A, D_A + D_B

_F32 = jnp.float32
_BF16 = jnp.bfloat16


def _layer_norm(x, g, b):
    mu = jnp.mean(x, axis=-1, keepdims=True)
    xc = x - mu
    var = jnp.mean(xc * xc, axis=-1, keepdims=True)
    return xc * lax.rsqrt(var + LN_EPS) * g + b


def _dot(a, b):
    return jnp.dot(a, b, preferred_element_type=_F32)


def _first_step_init(first, ws_ref, wsm_ref, *zero_refs):
    def mask_pair(p, carry):
        i_blk = lax.broadcasted_iota(jnp.int32, (MLP_CHUNK, 2 * MLP_CHUNK), 0) // CHUNK
        j_blk = (lax.broadcasted_iota(jnp.int32, (MLP_CHUNK, 2 * MLP_CHUNK), 1) % MLP_CHUNK) // CHUNK
        wsm_ref[p] = jnp.where(j_blk <= i_blk, ws_ref[p], 0.0).astype(_BF16)
        return carry

    lax.fori_loop(0, jnp.where(first, N_PAIRS_A, 0), mask_pair, 0)
    for ref in zero_refs:
        def zero_rows(i, carry, ref=ref):
            ref[pl.ds(pl.multiple_of(i * 8, 8), 8), :] = jnp.zeros((8, ref.shape[1]), ref.dtype)
            return carry

        lax.fori_loop(0, jnp.where(first, ref.shape[0] // 8, 0), zero_rows, 0)


def _layer_step(x_ref, w_in_ref, ws_ref, bz_ref, vg_ref, vb_ref, wpool_ref, pscale_ref, convw_ref,
                w_out_ref, g1_ref, b1_ref, wgu_ref, wd_ref, g2_ref, b2_ref,
                y_ref, v_out_ref, proj_ref, cat_ref, m_ref, x1b_ref, hid_ref, ffn_ref, wsm_ref,
                *, pipelined, seg_len, pos0, pool_prefix, conv_prefix):
    tile_rows = x_ref.shape[0]
    n_seg = tile_rows // seg_len
    rows_a = min(MLP_CHUNK, seg_len)

    def in_proj():
        proj_ref[...] = _dot(x_ref[...].astype(_BF16), w_in_ref[...])

    def chunk_mlp(c):
        low_half = lax.broadcasted_iota(jnp.int32, (rows_a, LANES), 1) < HEAD_DIM
        zero_rows = jnp.zeros((MLP_CHUNK - rows_a, LANES), _BF16)
        rows = pl.ds(c * rows_a, rows_a)
        v = _layer_norm(proj_ref[rows, _V0:_V0 + D_A], vg_ref[...], vb_ref[...])
        if v_out_ref is not None:
            v_out_ref[rows, :] = v
        for p in range(N_PAIRS_A):
            lanes = slice(p * LANES, (p + 1) * LANES)
            vp = v[:, lanes]
            pieces = [jnp.where(low_half, vp, 0.0).astype(_BF16)]
            if rows_a < MLP_CHUNK:
                pieces.append(zero_rows)
            pieces.append(jnp.where(low_half, 0.0, vp).astype(_BF16))
            if rows_a < MLP_CHUNK:
                pieces.append(zero_rows)
            rhs = jnp.concatenate(pieces, axis=0)
            z = _dot(wsm_ref[p, :rows_a, :], rhs) + bz_ref[:rows_a, lanes]
            u = proj_ref[rows, _U0 + p * LANES:_U0 + (p + 1) * LANES]
            cat_ref[rows, _OA0 + p * LANES:_OA0 + (p + 1) * LANES] = (u * z).astype(_BF16)

    def pool_conv(s):
        low_half = lax.broadcasted_iota(jnp.int32, (seg_len, LANES), 1) < POOL_GROUP_DIM
        pos = pos0 + lax.broadcasted_iota(jnp.int32, (seg_len, LANES), 0)
        rows = pl.ds(s * seg_len, seg_len)
        pre = pool_prefix(s)
        for half in range(D_B // LANES):
            lanes = slice(half * LANES, (half + 1) * LANES)
            xb = proj_ref[rows, _XB0 + half * LANES:_XB0 + (half + 1) * LANES]
            padded = jnp.concatenate([pre[:, lanes], xb], axis=0)
            s2 = padded + pltpu.roll(padded, 1, 0)
            s4 = s2 + pltpu.roll(s2, 2, 0)
            if half == 0:
                win_a, win_b, w_a, w_b = s2, s4, POOL_WINDOWS[0], POOL_WINDOWS[1]
            else:
                s8 = s4 + pltpu.roll(s4, 4, 0)
                s16 = s8 + pltpu.roll(s8, 8, 0)
                win_a, win_b, w_a, w_b = s8, s16, POOL_WINDOWS[2], POOL_WINDOWS[3]
            win = jnp.where(low_half, win_a[POOL_PAD:], win_b[POOL_PAD:])
            cnt = jnp.where(low_half, jnp.minimum(pos + 1, w_a), jnp.minimum(pos + 1, w_b)).astype(_F32)
            cat_ref[rows, _OB0 + half * LANES:_OB0 + (half + 1) * LANES] = (win / cnt - xb).astype(_BF16)

        zc = proj_ref[rows, _GC0:_GC0 + D_C] * proj_ref[rows, _H0:_H0 + D_C]
        padded = jnp.concatenate([conv_prefix(s), zc], axis=0)
        y = (convw_ref[0:1, :] * pltpu.roll(padded, 2, 0)[CONV_PAD:]
             + convw_ref[1:2, :] * pltpu.roll(padded, 1, 0)[CONV_PAD:]
             + convw_ref[2:3, :] * zc)
        cat_ref[rows, _OC0:_OC0 + D_C] = (proj_ref[rows, _GB0:_GB0 + D_C] * y).astype(_BF16)

    def pool_map():
        mapped = _dot(cat_ref[:, _OB0:_OB0 + D_B], wpool_ref[...]) * pscale_ref[...]
        cat_ref[:, _OB0:_OB0 + D_B] = mapped.astype(_BF16)

    def out_proj():
        m_ref[...] = ALPHA * x_ref[...] + _dot(cat_ref[...], w_out_ref[...])

    def mix_norm(r):
        rows = pl.ds(r * ROW_BLOCK, ROW_BLOCK)
        x1 = _layer_norm(m_ref[rows, :], g1_ref[...], b1_ref[...])
        y_ref[rows, :] = x1
        x1b_ref[rows, :] = x1.astype(_BF16)

    def ffn_hidden(c, blk=0, n_rows=None):
        rows = pl.ds(blk * (n_rows or tile_rows), n_rows or tile_rows)
        cols = slice(c * FF_BLOCK, (c + 1) * FF_BLOCK)
        gate_up = _dot(x1b_ref[rows, :], wgu_ref[:, 2 * c * FF_BLOCK:2 * (c + 1) * FF_BLOCK])
        gate, up = gate_up[:, :FF_BLOCK], gate_up[:, FF_BLOCK:]
        hid_ref[rows, cols] = (gate * (1.0 / (1.0 + jnp.exp(-gate))) * up).astype(_BF16)

    def ffn_down(blk):
        rows = pl.ds(blk * FFN_ROWS, FFN_ROWS)
        ffn_ref[rows, :] = ALPHA * y_ref[rows, :] + _dot(hid_ref[rows, :], wd_ref[...])

    def ffn_norm(r):
        rows = pl.ds(r * ROW_BLOCK, ROW_BLOCK)
        y_ref[rows, :] = _layer_norm(ffn_ref[rows, :], g2_ref[...], b2_ref[...])

    mixers = ([functools.partial(chunk_mlp, c) for c in range(tile_rows // rows_a)]
              + [functools.partial(pool_conv, s) for s in range(n_seg)] + [pool_map])
    mix_norms = [functools.partial(mix_norm, r) for r in range(tile_rows // ROW_BLOCK)]
    ffn_norms = [functools.partial(ffn_norm, r) for r in range(tile_rows // ROW_BLOCK)]
    n_ff = D_FF // FF_BLOCK

    if not pipelined:
        order = ([in_proj] + mixers + [out_proj] + mix_norms
                 + [functools.partial(ffn_hidden, c) for c in range(n_ff)]
                 + [functools.partial(ffn_down, blk) for blk in range(tile_rows // FFN_ROWS)] + ffn_norms)
        for piece in order:
            piece()
        return

    n_down = tile_rows // FFN_ROWS
    per_block = len(ffn_norms) // n_down
    in_proj()
    done = 0
    for piece in mix_norms[:per_block]:
        piece()
    later_norms = list(range(per_block, len(mix_norms)))
    for blk in range(n_down):
        for c in range(n_ff):
            ffn_hidden(c, blk, FFN_ROWS)
            if blk == 0 and c >= 1 and later_norms:
                r = later_norms.pop(0)
                bits = pltpu.bitcast(hid_ref[0:16, c * FF_BLOCK:c * FF_BLOCK + LANES], jnp.uint32)
                zero = lax.shift_right_logical(lax.shift_right_logical(bits, jnp.uint32(16)), jnp.uint32(16))
                first = (pl.ds(r * ROW_BLOCK, 8), pl.ds(0, LANES))
                m_ref[first] = m_ref[first] + pltpu.bitcast(zero, _F32)
                mix_norms[r]()
            upto = (len(mixers) * (blk * n_ff + c + 1)) // (n_down * n_ff)
            for piece in mixers[done:upto]:
                piece()
            done = upto
    assert not later_norms and done == len(mixers)
    for blk in range(n_down):
        ffn_down(blk)
        if blk == n_down - 1:
            out_proj()
        for piece in ffn_norms[blk * per_block:(blk + 1) * per_block]:
            piece()


def _state_tails(proj_ref, s, seg_len):
    pool_tail = proj_ref[pl.ds((s + 1) * seg_len - POOL_PAD, POOL_PAD), _XB0:_XB0 + D_B]
    rows = pl.ds((s + 1) * seg_len - CONV_PAD, CONV_PAD)
    return pool_tail, proj_ref[rows, _GC0:_GC0 + D_C] * proj_ref[rows, _H0:_H0 + D_C]


_N_WEIGHTS = 15


def _prompt_kernel(x_ref, *refs, n_tiles, n_steps):
    weights = refs[:_N_WEIGHTS]
    y_ref, pool_out_ref, conv_out_ref = refs[_N_WEIGHTS:_N_WEIGHTS + 3]
    (proj_ref, cat_ref, m_ref, x1b_ref, hid_ref, ffn_ref, wsm_ref,
     pool_carry_ref, conv_carry_ref) = refs[_N_WEIGHTS + 3:]
    step = pl.program_id(0)
    t = jnp.minimum(step, n_steps - 1) % n_tiles
    tile_rows = x_ref.shape[0]

    _first_step_init(step == 0, weights[1], wsm_ref, m_ref)

    @pl.when(t == 0)
    def _():
        pool_carry_ref[...] = jnp.zeros_like(pool_carry_ref)
        conv_carry_ref[...] = jnp.zeros_like(conv_carry_ref)

    _layer_step(x_ref, *weights, y_ref, None, proj_ref, cat_ref, m_ref, x1b_ref, hid_ref, ffn_ref, wsm_ref,
                pipelined=True, seg_len=tile_rows, pos0=t * tile_rows,
                pool_prefix=lambda s: pool_carry_ref[...], conv_prefix=lambda s: conv_carry_ref[...])

    @pl.when(step < n_steps)
    def _():
        pool_tail, conv_tail = _state_tails(proj_ref, 0, tile_rows)
        pool_carry_ref[...] = pool_tail
        conv_carry_ref[...] = conv_tail
        pool_out_ref[0] = pool_tail
        conv_out_ref[0] = conv_tail


def _sample_kernel(x_ref, pool_pre_ref, conv_pre_ref, *refs, seg_len):
    weights = refs[:_N_WEIGHTS]
    y_ref, pool_out_ref, conv_out_ref, v_out_ref = refs[_N_WEIGHTS:_N_WEIGHTS + 4]
    proj_ref, cat_ref, m_ref, x1b_ref, hid_ref, ffn_ref, wsm_ref = refs[_N_WEIGHTS + 4:]

    _first_step_init(pl.program_id(0) == 0, weights[1], wsm_ref)

    _layer_step(x_ref, *weights, y_ref, v_out_ref, proj_ref, cat_ref, m_ref, x1b_ref, hid_ref, ffn_ref, wsm_ref,
                pipelined=False, seg_len=seg_len, pos0=PAST_LEN,
                pool_prefix=lambda s: pool_pre_ref[s], conv_prefix=lambda s: conv_pre_ref[s])

    for s in range(x_ref.shape[0] // seg_len):
        pool_out_ref[s], conv_out_ref[s] = _state_tails(proj_ref, s, seg_len)


def _resident(layer, shape):
    return pl.BlockSpec((None,) + tuple(shape[1:]), lambda *_: (layer,) + (0,) * (len(shape) - 1),
                        pipeline_mode=pl.Buffered(1))


def _tile_scratch(tile_rows):
    return [
        pltpu.VMEM((tile_rows, D_IN), _F32),
        pltpu.VMEM((tile_rows, D_MODEL), _BF16),
        pltpu.VMEM((tile_rows, D_MODEL), _F32),
        pltpu.VMEM((tile_rows, D_MODEL), _BF16),
        pltpu.VMEM((tile_rows, D_FF), _BF16),
        pltpu.VMEM((tile_rows, D_MODEL), _F32),
        pltpu.VMEM((N_PAIRS_A, MLP_CHUNK, 2 * MLP_CHUNK), _BF16),
    ]


def _prompt_layer(layer, x, weights):
    batch, seq, _ = x.shape
    tile_rows = min(TILE_ROWS, seq)
    n_tiles = seq // tile_rows
    n_steps = batch * n_tiles
    assert seq % tile_rows == 0 and tile_rows % MLP_CHUNK == 0
    assert tile_rows % FFN_ROWS == 0 and FFN_ROWS % ROW_BLOCK == 0
    mix_tile = lambda i: jnp.minimum(i, n_steps - 1)
    ffn_tile = lambda i: jnp.maximum(i - 1, 0)
    y, pool_state, conv_state = pl.pallas_call(
        functools.partial(_prompt_kernel, n_tiles=n_tiles, n_steps=n_steps),
        grid=(n_steps + 1,),
        in_specs=[pl.BlockSpec((tile_rows, D_MODEL), lambda i: (mix_tile(i), 0))]
                 + [_resident(layer, w.shape) for w in weights],
        out_specs=[pl.BlockSpec((tile_rows, D_MODEL), lambda i: (ffn_tile(i), 0)),
                   pl.BlockSpec((1, POOL_PAD, D_B), lambda i: (mix_tile(i) // n_tiles, 0, 0)),
                   pl.BlockSpec((1, CONV_PAD, D_C), lambda i: (mix_tile(i) // n_tiles, 0, 0))],
        out_shape=[jax.ShapeDtypeStruct((batch * seq, D_MODEL), _F32),
                   jax.ShapeDtypeStruct((batch, POOL_PAD, D_B), _F32),
                   jax.ShapeDtypeStruct((batch, CONV_PAD, D_C), _F32)],
        scratch_shapes=_tile_scratch(tile_rows) + [pltpu.VMEM((POOL_PAD, D_B), _F32),
                                                   pltpu.VMEM((CONV_PAD, D_C), _F32)],
        compiler_params=pltpu.CompilerParams(dimension_semantics=("arbitrary",),
                                             vmem_limit_bytes=VMEM_LIMIT_BYTES,
                                             allow_input_fusion=[False] + [w.dtype == _BF16 for w in weights]),
        name="prompt_layer",
    )(x.reshape(batch * seq, D_MODEL), *weights)
    return y.reshape(batch, seq, D_MODEL), pool_state[:, POOL_PAD - POOL_STATE:], conv_state[:, CONV_PAD - (CONV_W - 1):]


def _sample_layer(layer, x, pool_prefix, conv_prefix, weights):
    batch, seq, _ = x.shape
    assert seq % POOL_PAD == 0 and (seq % MLP_CHUNK == 0 or MLP_CHUNK % seq == 0 and seq % CHUNK == 0)
    seg_per_tile = max(1, min(batch, TILE_ROWS // seq))
    assert batch % seg_per_tile == 0
    tile_rows = seg_per_tile * seq
    assert tile_rows % FFN_ROWS == 0 and FFN_ROWS % ROW_BLOCK == 0
    pool_pre = jnp.pad(pool_prefix, ((0, 0), (POOL_PAD - POOL_STATE, 0), (0, 0)))
    conv_pre = jnp.pad(conv_prefix, ((0, 0), (CONV_PAD - (CONV_W - 1), 0), (0, 0)))
    y, pool_state, conv_state, v = pl.pallas_call(
        functools.partial(_sample_kernel, seg_len=seq),
        grid=(batch // seg_per_tile,),
        in_specs=[pl.BlockSpec((tile_rows, D_MODEL), lambda i: (i, 0)),
                  pl.BlockSpec((seg_per_tile, POOL_PAD, D_B), lambda i: (i, 0, 0)),
                  pl.BlockSpec((seg_per_tile, CONV_PAD, D_C), lambda i: (i, 0, 0))]
                 + [_resident(layer, w.shape) for w in weights],
        out_specs=[pl.BlockSpec((tile_rows, D_MODEL), lambda i: (i, 0)),
                   pl.BlockSpec((seg_per_tile, POOL_PAD, D_B), lambda i: (i, 0, 0)),
                   pl.BlockSpec((seg_per_tile, CONV_PAD, D_C), lambda i: (i, 0, 0)),
                   pl.BlockSpec((tile_rows, D_A), lambda i: (i, 0))],
        out_shape=[jax.ShapeDtypeStruct((batch * seq, D_MODEL), _F32),
                   jax.ShapeDtypeStruct((batch, POOL_PAD, D_B), _F32),
                   jax.ShapeDtypeStruct((batch, CONV_PAD, D_C), _F32),
                   jax.ShapeDtypeStruct((batch * seq, D_A), _F32)],
        scratch_shapes=_tile_scratch(tile_rows),
        compiler_params=pltpu.CompilerParams(dimension_semantics=("arbitrary",),
                                             vmem_limit_bytes=VMEM_LIMIT_BYTES),
        name="sample_layer",
    )(x.reshape(batch * seq, D_MODEL), pool_pre, conv_pre, *weights)
    return (y.reshape(batch, seq, D_MODEL), pool_state[:, POOL_PAD - POOL_STATE:],
            conv_state[:, CONV_PAD - (CONV_W - 1):], v.reshape(batch, seq, D_A))


def _stacked_weights(ln_mix_g, ln_mix_b, w_in, v_norm_g, v_norm_b, w_spatial, b_spatial, w_pool, pool_scale,
                     conv_w, w_out, ln_ffn_g, ln_ffn_b, w_gate, w_up, w_down):
    depth = w_in.shape[0]
    row = lambda a: a.reshape(depth, 1, -1)
    ws = w_spatial.reshape(depth, N_PAIRS_A, 2, MLP_CHUNK, MLP_CHUNK).transpose(0, 1, 3, 2, 4)
    ws = ws.reshape(depth, N_PAIRS_A, MLP_CHUNK, 2 * MLP_CHUNK)
    bz = jnp.repeat(b_spatial.transpose(0, 2, 1), HEAD_DIM, axis=2)
    n_groups = len(POOL_WINDOWS)
    same_group = jnp.eye(n_groups, dtype=w_pool.dtype)[None, :, None, :, None]
    wpool = (w_pool[:, :, :, None, :] * same_group).reshape(depth, D_B, D_B).astype(_BF16)
    n_ff = D_FF // FF_BLOCK
    w_gu = jnp.stack([w_gate.astype(_BF16).reshape(depth, D_MODEL, n_ff, FF_BLOCK),
                      w_up.astype(_BF16).reshape(depth, D_MODEL, n_ff, FF_BLOCK)], axis=3)
    w_gu = w_gu.reshape(depth, D_MODEL, 2 * D_FF)
    return (w_in.astype(_BF16), ws, bz, row(v_norm_g), row(v_norm_b), wpool, row(pool_scale),
            conv_w, w_out.astype(_BF16), row(ln_mix_g), row(ln_mix_b),
            w_gu, w_down.astype(_BF16), row(ln_ffn_g), row(ln_ffn_b))


def kernel(x_prompt, x_sample, state_pool, state_conv, ln_mix_g, ln_mix_b, w_in, v_norm_g, v_norm_b, w_spatial, b_spatial, w_pool, pool_scale, conv_w, w_out, ln_ffn_g, ln_ffn_b, w_gate, w_up, w_down):
    weights = _stacked_weights(ln_mix_g, ln_mix_b, w_in, v_norm_g, v_norm_b, w_spatial, b_spatial, w_pool,
                               pool_scale, conv_w, w_out, ln_ffn_g, ln_ffn_b, w_gate, w_up, w_down)
    yp, ys = x_prompt, x_sample
    pool_p, pool_s, conv_p, conv_s, v_s = [], [], [], [], []
    for l in range(DEPTH):
        yp, pp, cp = _prompt_layer(l, yp, weights)
        ys, ps, cs, vs = _sample_layer(l, ys, state_pool[l], state_conv[l], weights)
        pool_p.append(pp)
        conv_p.append(cp)
        pool_s.append(ps)
        conv_s.append(cs)
        v_s.append(vs)
    return (yp, ys, jnp.stack(pool_p), jnp.stack(pool_s), jnp.stack(conv_p), jnp.stack(conv_s), jnp.stack(v_s))
```

```python
import functools

import jax
import jax.numpy as jnp
from jax import lax
from jax.experimental import pallas as pl
from jax.experimental.pallas import tpu as pltpu

D_MODEL = 1024
DEPTH = 2
PAST_LEN = 4096
CHUNK = 64
MLP_CHUNK = 128
HEAD_DIM = 64
D_A = 512
N_HEADS_A = D_A // HEAD_DIM
N_PAIRS_A = N_HEADS_A // 2
D_B = 256
POOL_WINDOWS = (2, 4, 8, 16)
POOL_GROUP_DIM = D_B // len(POOL_WINDOWS)
POOL_STATE = max(POOL_WINDOWS) - 1
D_C = 256
CONV_W = 3
D_IN = 2 * D_A + D_B + 3 * D_C
D_FF = 2816
ALPHA = (2 * DEPTH) ** 0.25
LN_EPS = 1e-5

LANES = 128
POOL_PAD = 16
CONV_PAD = 8
FF_BLOCK = 256
ROW_BLOCK = 64
FFN_ROWS = 256
TILE_ROWS = 512
VMEM_LIMIT_BYTES = 50 * 1024 * 1024

_U0, _V0, _XB0, _GB0, _GC0, _H0 = 0, D_A, 2 * D_A, 2 * D_A + D_B, 2 * D_A + D_B + D_C, 2 * D_A + D_B + 2 * D_C
_OA0, _OB0, _OC0 = 0, D_A, D_A + D_B

_F32 = jnp.float32
_BF16 = jnp.bfloat16


def _layer_norm(x, g, b):
    mu = jnp.mean(x, axis=-1, keepdims=True)
    xc = x - mu
    var = jnp.mean(xc * xc, axis=-1, keepdims=True)
    return xc * lax.rsqrt(var + LN_EPS) * g + b


def _dot(a, b):
    return jnp.dot(a, b, preferred_element_type=_F32)


def _first_step_init(first, ws_ref, wsm_ref, *zero_refs):
    def mask_pair(p, carry):
        i_blk = lax.broadcasted_iota(jnp.int32, (MLP_CHUNK, 2 * MLP_CHUNK), 0) // CHUNK
        j_blk = (lax.broadcasted_iota(jnp.int32, (MLP_CHUNK, 2 * MLP_CHUNK), 1) % MLP_CHUNK) // CHUNK
        wsm_ref[p] = jnp.where(j_blk <= i_blk, ws_ref[p], 0.0).astype(_BF16)
        return carry

    lax.fori_loop(0, jnp.where(first, N_PAIRS_A, 0), mask_pair, 0)
    for ref in zero_refs:
        def zero_rows(i, carry, ref=ref):
            ref[pl.ds(pl.multiple_of(i * 8, 8), 8), :] = jnp.zeros((8, ref.shape[1]), ref.dtype)
            return carry

        lax.fori_loop(0, jnp.where(first, ref.shape[0] // 8, 0), zero_rows, 0)


def _layer_step(x_ref, w_in_ref, ws_ref, bz_ref, vg_ref, vb_ref, wpool_ref, pscale_ref, convw_ref,
                w_out_ref, g1_ref, b1_ref, wg_ref, wu_ref, wd_ref, g2_ref, b2_ref,
                y_ref, v_out_ref, proj_ref, cat_ref, m_ref, x1b_ref, hid_ref, ffn_ref, wsm_ref,
                *, pipelined, seg_len, pos0, pool_prefix, conv_prefix):
    tile_rows = x_ref.shape[0]
    n_seg = tile_rows // seg_len
    rows_a = min(MLP_CHUNK, seg_len)

    def in_proj():
        proj_ref[...] = _dot(x_ref[...].astype(_BF16), w_in_ref[...])

    def chunk_mlp(c):
        low_half = lax.broadcasted_iota(jnp.int32, (rows_a, LANES), 1) < HEAD_DIM
        zero_rows = jnp.zeros((MLP_CHUNK - rows_a, LANES), _BF16)
        rows = pl.ds(c * rows_a, rows_a)
        v = _layer_norm(proj_ref[rows, _V0:_V0 + D_A], vg_ref[...], vb_ref[...])
        if v_out_ref is not None:
            v_out_ref[rows, :] = v
        for p in range(N_PAIRS_A):
            lanes = slice(p * LANES, (p + 1) * LANES)
            vp = v[:, lanes]
            pieces = [jnp.where(low_half, vp, 0.0).astype(_BF16)]
            if rows_a < MLP_CHUNK:
                pieces.append(zero_rows)
            pieces.append(jnp.where(low_half, 0.0, vp).astype(_BF16))
            if rows_a < MLP_CHUNK:
                pieces.append(zero_rows)
            rhs = jnp.concatenate(pieces, axis=0)
            z = _dot(wsm_ref[p, :rows_a, :], rhs) + bz_ref[:rows_a, lanes]
            u = proj_ref[rows, _U0 + p * LANES:_U0 + (p + 1) * LANES]
            cat_ref[rows, _OA0 + p * LANES:_OA0 + (p + 1) * LANES] = (u * z).astype(_BF16)

    def pool_conv(s):
        low_half = lax.broadcasted_iota(jnp.int32, (seg_len, LANES), 1) < POOL_GROUP_DIM
        pos = pos0 + lax.broadcasted_iota(jnp.int32, (seg_len, LANES), 0)
        rows = pl.ds(s * seg_len, seg_len)
        pre = pool_prefix(s)
        for half in range(D_B // LANES):
            lanes = slice(half * LANES, (half + 1) * LANES)
            xb = proj_ref[rows, _XB0 + half * LANES:_XB0 + (half + 1) * LANES]
            padded = jnp.concatenate([pre[:, lanes], xb], axis=0)
            s2 = padded + pltpu.roll(padded, 1, 0)
            s4 = s2 + pltpu.roll(s2, 2, 0)
            if half == 0:
                win_a, win_b, w_a, w_b = s2, s4, POOL_WINDOWS[0], POOL_WINDOWS[1]
            else:
                s8 = s4 + pltpu.roll(s4, 4, 0)
                s16 = s8 + pltpu.roll(s8, 8, 0)
                win_a, win_b, w_a, w_b = s8, s16, POOL_WINDOWS[2], POOL_WINDOWS[3]
            win = jnp.where(low_half, win_a[POOL_PAD:], win_b[POOL_PAD:])
            cnt = jnp.where(low_half, jnp.minimum(pos + 1, w_a), jnp.minimum(pos + 1, w_b)).astype(_F32)
            cat_ref[rows, _OB0 + half * LANES:_OB0 + (half + 1) * LANES] = (win / cnt - xb).astype(_BF16)

        zc = proj_ref[rows, _GC0:_GC0 + D_C] * proj_ref[rows, _H0:_H0 + D_C]
        padded = jnp.concatenate([conv_prefix(s), zc], axis=0)
        y = (convw_ref[0:1, :] * pltpu.roll(padded, 2, 0)[CONV_PAD:]
             + convw_ref[1:2, :] * pltpu.roll(padded, 1, 0)[CONV_PAD:]
             + convw_ref[2:3, :] * zc)
        cat_ref[rows, _OC0:_OC0 + D_C] = (proj_ref[rows, _GB0:_GB0 + D_C] * y).astype(_BF16)

    def pool_map():
        mapped = _dot(cat_ref[:, _OB0:_OB0 + D_B], wpool_ref[...]) * pscale_ref[...]
        cat_ref[:, _OB0:_OB0 + D_B] = mapped.astype(_BF16)

    def out_proj():
        m_ref[...] = ALPHA * x_ref[...] + _dot(cat_ref[...], w_out_ref[...])

    def mix_norm(r):
        rows = pl.ds(r * ROW_BLOCK, ROW_BLOCK)
        x1 = _layer_norm(m_ref[rows, :], g1_ref[...], b1_ref[...])
        y_ref[rows, :] = x1
        x1b_ref[rows, :] = x1.astype(_BF16)

    def ffn_hidden(c, blk=0, n_rows=None):
        rows = pl.ds(blk * (n_rows or tile_rows), n_rows or tile_rows)
        cols = slice(c * FF_BLOCK, (c + 1) * FF_BLOCK)
        gate = _dot(x1b_ref[rows, :], wg_ref[:, cols])
        up = _dot(x1b_ref[rows, :], wu_ref[:, cols])
        hid_ref[rows, cols] = (gate * (1.0 / (1.0 + jnp.exp(-gate))) * up).astype(_BF16)

    def ffn_down(blk):
        rows = pl.ds(blk * FFN_ROWS, FFN_ROWS)
        ffn_ref[rows, :] = ALPHA * y_ref[rows, :] + _dot(hid_ref[rows, :], wd_ref[...])

    def ffn_norm(r):
        rows = pl.ds(r * ROW_BLOCK, ROW_BLOCK)
        y_ref[rows, :] = _layer_norm(ffn_ref[rows, :], g2_ref[...], b2_ref[...])

    mixers = ([functools.partial(chunk_mlp, c) for c in range(tile_rows // rows_a)]
              + [functools.partial(pool_conv, s) for s in range(n_seg)] + [pool_map])
    mix_norms = [functools.partial(mix_norm, r) for r in range(tile_rows // ROW_BLOCK)]
    ffn_norms = [functools.partial(ffn_norm, r) for r in range(tile_rows // ROW_BLOCK)]
    n_ff = D_FF // FF_BLOCK

    if not pipelined:
        order = ([in_proj] + mixers + [out_proj] + mix_norms
                 + [functools.partial(ffn_hidden, c) for c in range(n_ff)]
                 + [functools.partial(ffn_down, blk) for blk in range(tile_rows // FFN_ROWS)] + ffn_norms)
        for piece in order:
            piece()
        return

    n_down = tile_rows // FFN_ROWS
    per_block = len(ffn_norms) // n_down
    in_proj()
    done = 0
    for piece in mix_norms[:per_block]:
        piece()
    later_norms = list(range(per_block, len(mix_norms)))
    for blk in range(n_down):
        for c in range(n_ff):
            ffn_hidden(c, blk, FFN_ROWS)
            if blk == 0 and c >= 1 and later_norms:
                r = later_norms.pop(0)
                bits = pltpu.bitcast(hid_ref[0:16, c * FF_BLOCK:c * FF_BLOCK + LANES], jnp.uint32)
                zero = lax.shift_right_logical(lax.shift_right_logical(bits, jnp.uint32(16)), jnp.uint32(16))
                first = (pl.ds(r * ROW_BLOCK, 8), pl.ds(0, LANES))
                m_ref[first] = m_ref[first] + pltpu.bitcast(zero, _F32)
                mix_norms[r]()
            upto = (len(mixers) * (blk * n_ff + c + 1)) // (n_down * n_ff)
            for piece in mixers[done:upto]:
                piece()
            done = upto
    assert not later_norms and done == len(mixers)
    for blk in range(n_down):
        ffn_down(blk)
        if blk == n_down - 1:
            out_proj()
        for piece in ffn_norms[blk * per_block:(blk + 1) * per_block]:
            piece()


def _state_tails(proj_ref, s, seg_len):
    pool_tail = proj_ref[pl.ds((s + 1) * seg_len - POOL_PAD, POOL_PAD), _XB0:_XB0 + D_B]
    rows = pl.ds((s + 1) * seg_len - CONV_PAD, CONV_PAD)
    return pool_tail, proj_ref[rows, _GC0:_GC0 + D_C] * proj_ref[rows, _H0:_H0 + D_C]


_N_WEIGHTS = 16


def _prompt_kernel(x_ref, *refs, n_tiles, n_steps):
    weights = refs[:_N_WEIGHTS]
    y_ref, pool_out_ref, conv_out_ref = refs[_N_WEIGHTS:_N_WEIGHTS + 3]
    (proj_ref, cat_ref, m_ref, x1b_ref, hid_ref, ffn_ref, wsm_ref,
     pool_carry_ref, conv_carry_ref) = refs[_N_WEIGHTS + 3:]
    step = pl.program_id(0)
    t = jnp.minimum(step, n_steps - 1) % n_tiles
    tile_rows = x_ref.shape[0]

    _first_step_init(step == 0, weights[1], wsm_ref, m_ref)

    @pl.when(t == 0)
    def _():
        pool_carry_ref[...] = jnp.zeros_like(pool_carry_ref)
        conv_carry_ref[...] = jnp.zeros_like(conv_carry_ref)

    _layer_step(x_ref, *weights, y_ref, None, proj_ref, cat_ref, m_ref, x1b_ref, hid_ref, ffn_ref, wsm_ref,
                pipelined=True, seg_len=tile_rows, pos0=t * tile_rows,
                pool_prefix=lambda s: pool_carry_ref[...], conv_prefix=lambda s: conv_carry_ref[...])

    @pl.when(step < n_steps)
    def _():
        pool_tail, conv_tail = _state_tails(proj_ref, 0, tile_rows)
        pool_carry_ref[...] = pool_tail
        conv_carry_ref[...] = conv_tail
        pool_out_ref[0] = pool_tail
        conv_out_ref[0] = conv_tail


def _sample_kernel(x_ref, pool_pre_ref, conv_pre_ref, *refs, seg_len):
    weights = refs[:_N_WEIGHTS]
    y_ref, pool_out_ref, conv_out_ref, v_out_ref = refs[_N_WEIGHTS:_N_WEIGHTS + 4]
    proj_ref, cat_ref, m_ref, x1b_ref, hid_ref, ffn_ref, wsm_ref = refs[_N_WEIGHTS + 4:]

    _first_step_init(pl.program_id(0) == 0, weights[1], wsm_ref)

    _layer_step(x_ref, *weights, y_ref, v_out_ref, proj_ref, cat_ref, m_ref, x1b_ref, hid_ref, ffn_ref, wsm_ref,
                pipelined=False, seg_len=seg_len, pos0=PAST_LEN,
                pool_prefix=lambda s: pool_pre_ref[s], conv_prefix=lambda s: conv_pre_ref[s])

    for s in range(x_ref.shape[0] // seg_len):
        pool_out_ref[s], conv_out_ref[s] = _state_tails(proj_ref, s, seg_len)


def _resident(layer, shape):
    return pl.BlockSpec((None,) + tuple(shape[1:]), lambda *_: (layer,) + (0,) * (len(shape) - 1),
                        pipeline_mode=pl.Buffered(1))


def _tile_scratch(tile_rows):
    return [
        pltpu.VMEM((tile_rows, D_IN), _F32),
        pltpu.VMEM((tile_rows, D_MODEL), _BF16),
        pltpu.VMEM((tile_rows, D_MODEL), _F32),
        pltpu.VMEM((tile_rows, D_MODEL), _BF16),
        pltpu.VMEM((tile_rows, D_FF), _BF16),
        pltpu.VMEM((tile_rows, D_MODEL), _F32),
        pltpu.VMEM((N_PAIRS_A, MLP_CHUNK, 2 * MLP_CHUNK), _BF16),
    ]


def _prompt_layer(layer, x, weights):
    batch, seq, _ = x.shape
    tile_rows = min(TILE_ROWS, seq)
    n_tiles = seq // tile_rows
    n_steps = batch * n_tiles
    assert seq % tile_rows == 0 and tile_rows % MLP_CHUNK == 0
    assert tile_rows % FFN_ROWS == 0 and FFN_ROWS % ROW_BLOCK == 0
    mix_tile = lambda i: jnp.minimum(i, n_steps - 1)
    ffn_tile = lambda i: jnp.maximum(i - 1, 0)
    y, pool_state, conv_state = pl.pallas_call(
        functools.partial(_prompt_kernel, n_tiles=n_tiles, n_steps=n_steps),
        grid=(n_steps + 1,),
        in_specs=[pl.BlockSpec((tile_rows, D_MODEL), lambda i: (mix_tile(i), 0))]
                 + [_resident(layer, w.shape) for w in weights],
        out_specs=[pl.BlockSpec((tile_rows, D_MODEL), lambda i: (ffn_tile(i), 0)),
                   pl.BlockSpec((1, POOL_PAD, D_B), lambda i: (mix_tile(i) // n_tiles, 0, 0)),
                   pl.BlockSpec((1, CONV_PAD, D_C), lambda i: (mix_tile(i) // n_tiles, 0, 0))],
        out_shape=[jax.ShapeDtypeStruct((batch * seq, D_MODEL), _F32),
                   jax.ShapeDtypeStruct((batch, POOL_PAD, D_B), _F32),
                   jax.ShapeDtypeStruct((batch, CONV_PAD, D_C), _F32)],
        scratch_shapes=_tile_scratch(tile_rows) + [pltpu.VMEM((POOL_PAD, D_B), _F32),
                                                   pltpu.VMEM((CONV_PAD, D_C), _F32)],
        compiler_params=pltpu.CompilerParams(dimension_semantics=("arbitrary",),
                                             vmem_limit_bytes=VMEM_LIMIT_BYTES),
        name="prompt_layer",
    )(x.reshape(batch * seq, D_MODEL), *weights)
    return y.reshape(batch, seq, D_MODEL), pool_state[:, POOL_PAD - POOL_STATE:], conv_state[:, CONV_PAD - (CONV_W - 1):]


def _sample_layer(layer, x, pool_prefix, conv_prefix, weights):
    batch, seq, _ = x.shape
    assert seq % POOL_PAD == 0 and (seq % MLP_CHUNK == 0 or MLP_CHUNK % seq == 0 and seq % CHUNK == 0)
    seg_per_tile = max(1, min(batch, TILE_ROWS // seq))
    assert batch % seg_per_tile == 0
    tile_rows = seg_per_tile * seq
    assert tile_rows % FFN_ROWS == 0 and FFN_ROWS % ROW_BLOCK == 0
    pool_pre = jnp.pad(pool_prefix, ((0, 0), (POOL_PAD - POOL_STATE, 0), (0, 0)))
    conv_pre = jnp.pad(conv_prefix, ((0, 0), (CONV_PAD - (CONV_W - 1), 0), (0, 0)))
    y, pool_state, conv_state, v = pl.pallas_call(
        functools.partial(_sample_kernel, seg_len=seq),
        grid=(batch // seg_per_tile,),
        in_specs=[pl.BlockSpec((tile_rows, D_MODEL), lambda i: (i, 0)),
                  pl.BlockSpec((seg_per_tile, POOL_PAD, D_B), lambda i: (i, 0, 0)),
                  pl.BlockSpec((seg_per_tile, CONV_PAD, D_C), lambda i: (i, 0, 0))]
                 + [_resident(layer, w.shape) for w in weights],
        out_specs=[pl.BlockSpec((tile_rows, D_MODEL), lambda i: (i, 0)),
                   pl.BlockSpec((seg_per_tile, POOL_PAD, D_B), lambda i: (i, 0, 0)),
                   pl.BlockSpec((seg_per_tile, CONV_PAD, D_C), lambda i: (i, 0, 0)),
                   pl.BlockSpec((tile_rows, D_A), lambda i: (i, 0))],
        out_shape=[jax.ShapeDtypeStruct((batch * seq, D_MODEL), _F32),
                   jax.ShapeDtypeStruct((batch, POOL_PAD, D_B), _F32),
                   jax.ShapeDtypeStruct((batch, CONV_PAD, D_C), _F32),
                   jax.ShapeDtypeStruct((batch * seq, D_A), _F32)],
        scratch_shapes=_tile_scratch(tile_rows),
        compiler_params=pltpu.CompilerParams(dimension_semantics=("arbitrary",),
                                             vmem_limit_bytes=VMEM_LIMIT_BYTES),
        name="sample_layer",
    )(x.reshape(batch * seq, D_MODEL), pool_pre, conv_pre, *weights)
    return (y.reshape(batch, seq, D_MODEL), pool_state[:, POOL_PAD - POOL_STATE:],
            conv_state[:, CONV_PAD - (CONV_W - 1):], v.reshape(batch, seq, D_A))


def _stacked_weights(ln_mix_g, ln_mix_b, w_in, v_norm_g, v_norm_b, w_spatial, b_spatial, w_pool, pool_scale,
                     conv_w, w_out, ln_ffn_g, ln_ffn_b, w_gate, w_up, w_down):
    depth = w_in.shape[0]
    row = lambda a: a.reshape(depth, 1, -1)
    ws = w_spatial.reshape(depth, N_PAIRS_A, 2, MLP_CHUNK, MLP_CHUNK).transpose(0, 1, 3, 2, 4)
    ws = ws.reshape(depth, N_PAIRS_A, MLP_CHUNK, 2 * MLP_CHUNK)
    bz = jnp.repeat(b_spatial.transpose(0, 2, 1), HEAD_DIM, axis=2)
    n_groups = len(POOL_WINDOWS)
    same_group = jnp.eye(n_groups, dtype=w_pool.dtype)[None, :, None, :, None]
    wpool = (w_pool[:, :, :, None, :] * same_group).reshape(depth, D_B, D_B).astype(_BF16)
    return (w_in.astype(_BF16), ws, bz, row(v_norm_g), row(v_norm_b), wpool, row(pool_scale),
            conv_w, w_out.astype(_BF16), row(ln_mix_g), row(ln_mix_b),
            w_gate.astype(_BF16), w_up.astype(_BF16), w_down.astype(_BF16),
            row(ln_ffn_g), row(ln_ffn_b))


def kernel(x_prompt, x_sample, state_pool, state_conv, ln_mix_g, ln_mix_b, w_in, v_norm_g, v_norm_b, w_spatial, b_spatial, w_pool, pool_scale, conv_w, w_out, ln_ffn_g, ln_ffn_b, w_gate, w_up, w_down):
    weights = _stacked_weights(ln_mix_g, ln_mix_b, w_in, v_norm_g, v_norm_b, w_spatial, b_spatial, w_pool,
                               pool_scale, conv_w, w_out, ln_ffn_g, ln_ffn_b, w_gate, w_up, w_down)
    yp, ys = x_prompt, x_sample
    pool_p, pool_s, conv_p, conv_s, v_s = [], [], [], [], []
    for l in range(DEPTH):
        yp, pp, cp = _prompt_layer(l, yp, weights)
        ys, ps, cs, vs = _sample_layer(l, ys, state_pool[l], state_conv[l], weights)
        pool_p.append(pp)
        conv_p.append(cp)
        pool_s.append(ps)
        conv_s.append(cs)
        v_s.append(vs)
    return (yp, ys, jnp.stack(pool_p), jnp.stack(pool_s), jnp.stack(conv_p), jnp.stack(conv_s), jnp.stack(v_s))
```
